```python
import jax, jax.numpy as jnp
from jax import lax
import numpy as np

D_MODEL = 2048
BATCH = 8
SEQ = 2048
DEPTH = 1

MEM_LEN = 256
HEAD_DIM = 128
MOBA_HEADS = 8
MOBA_BLOCK = 256
MOBA_TOPK = 3
MOBA_Q_CHUNK = 16
DIL_CONFIGS = ((128, 1), (512, 4), (2048, 16))
DIL_HEADS_PER_GROUP = 4
MEM_HEADS = 4
MEM_HEAD_DIM = 256
ROPE_THETA = 10000.0
RMS_EPS = 1e-6
NEG = -1e30
N_BRANCH = 3

W_A = MOBA_HEADS * HEAD_DIM
W_B_QKV = len(DIL_CONFIGS) * DIL_HEADS_PER_GROUP * HEAD_DIM
W_B_OUT = DIL_HEADS_PER_GROUP * HEAD_DIM
W_M = MEM_HEADS * MEM_HEAD_DIM
IN_SPLITS = (W_A, W_A, W_A, W_A, W_B_QKV, W_B_QKV, W_B_QKV, W_B_OUT, W_M, W_M, N_BRANCH * D_MODEL)
IN_WIDTH = sum(IN_SPLITS)

kernel_name = "hybrid_moba_dilated_memory_gated"


def rms_norm(x, g):
    xf = x.astype(jnp.float32)
    y = xf * lax.rsqrt(jnp.mean(xf * xf, axis=-1, keepdims=True) + RMS_EPS)
    return (y * g.astype(jnp.float32)).astype(x.dtype)


def rope(x):
    S, D = x.shape[1], x.shape[-1]
    half = D // 2
    inv = ROPE_THETA ** (-jnp.arange(half, dtype=jnp.float32) / half)
    ang = jnp.arange(S, dtype=jnp.float32)[:, None] * inv[None, :]
    cos = jnp.cos(ang)[None, :, None, :]
    sin = jnp.sin(ang)[None, :, None, :]
    xf = x.astype(jnp.float32)
    x1, x2 = xf[..., :half], xf[..., half:]
    return jnp.concatenate([x1 * cos - x2 * sin, x2 * cos + x1 * sin], axis=-1).astype(x.dtype)


def moba_attention(q, k, v):
    B, S, H, D = q.shape
    C = MOBA_Q_CHUNK
    nb = -(-S // MOBA_BLOCK)
    s_pad = nb * MOBA_BLOCK
    scale = D ** -0.5
    qh = q.transpose(0, 2, 1, 3)
    pad = ((0, 0), (0, 0), (0, s_pad - S), (0, 0))
    kb = jnp.pad(k.transpose(0, 2, 1, 3), pad).reshape(B, H, nb, MOBA_BLOCK, D)
    vb = jnp.pad(v.transpose(0, 2, 1, 3), pad).reshape(B, H, nb, MOBA_BLOCK, D)
    k_mean = jnp.mean(kb.astype(jnp.float32), axis=3)
    gate = jnp.einsum('bhsd,bhnd->bhsn', qh.astype(jnp.float32), k_mean)
    q_blk = jnp.arange(S) // MOBA_BLOCK
    past = jnp.arange(nb)[None, :] < q_blk[:, None]
    gate = jnp.where(past[None, None], gate, -jnp.inf)
    n_sel = max(1, min(MOBA_TOPK, nb - 1))
    _, sel = lax.top_k(gate, n_sel)
    sel_valid = jnp.arange(n_sel)[None, :] < q_blk[:, None]

    nq = S // C
    qc = qh.reshape(B, H, nq, C, D).transpose(2, 0, 1, 3, 4)
    selc = sel.reshape(B, H, nq, C, n_sel).transpose(2, 0, 1, 3, 4)
    validc = sel_valid.reshape(nq, C, n_sel)
    bi = jnp.arange(B)[:, None, None, None]
    hi = jnp.arange(H)[None, :, None, None]

    def chunk(args):
        c, q_c, sel_c, valid_c = args
        start = c * C
        own = start // MOBA_BLOCK
        k_own = lax.dynamic_index_in_dim(kb, own, axis=2, keepdims=False)
        v_own = lax.dynamic_index_in_dim(vb, own, axis=2, keepdims=False)
        k_sel = kb[bi, hi, sel_c]
        v_sel = vb[bi, hi, sel_c]
        s_sel = jnp.einsum('bhqd,bhqnkd->bhqnk', q_c, k_sel,
                           preferred_element_type=jnp.float32) * scale
        s_sel = jnp.where(valid_c[None, None, :, :, None], s_sel, NEG)
        s_own = jnp.einsum('bhqd,bhkd->bhqk', q_c, k_own,
                           preferred_element_type=jnp.float32) * scale
        q_pos = start + jnp.arange(C)
        k_pos = own * MOBA_BLOCK + jnp.arange(MOBA_BLOCK)
        s_own = jnp.where(k_pos[None, :] <= q_pos[:, None], s_own, NEG)
        s_all = jnp.concatenate([s_sel.reshape(B, H, C, n_sel * MOBA_BLOCK), s_own], axis=-1)
        p = jax.nn.softmax(s_all, axis=-1)
        p_sel = p[..., :n_sel * MOBA_BLOCK].reshape(B, H, C, n_sel, MOBA_BLOCK).astype(v.dtype)
        p_own = p[..., n_sel * MOBA_BLOCK:].astype(v.dtype)
        return (jnp.einsum('bhqnk,bhqnkd->bhqd', p_sel, v_sel)
                + jnp.einsum('bhqk,bhkd->bhqd', p_own, v_own))

    out = lax.map(chunk, (jnp.arange(nq), qc, selc, validc))
    return out.transpose(1, 0, 3, 2, 4).reshape(B, S, H, D)


def dilated_group(q, k, v, window, dilation):
    B, S, H, D = q.shape
    band = window // dilation
    L = S // dilation
    nblk = -(-L // band)
    Lp = nblk * band
    scale = D ** -0.5

    def to_sub(t):
        t = t.reshape(B, L, dilation, H, D).transpose(0, 2, 3, 1, 4)
        t = jnp.pad(t, ((0, 0), (0, 0), (0, 0), (0, Lp - L), (0, 0)))
        return t.reshape(B, dilation, H, nblk, band, D)

    def with_prev(t):
        prev = jnp.concatenate([jnp.zeros_like(t[:, :, :, :1]), t[:, :, :, :-1]], axis=3)
        return jnp.concatenate([prev, t], axis=4)

    qs, ks, vs = to_sub(q), with_prev(to_sub(k)), with_prev(to_sub(v))
    s = jnp.einsum('brhnqd,brhnkd->brhnqk', qs, ks, preferred_element_type=jnp.float32) * scale
    n_i = jnp.arange(nblk)[:, None, None]
    q_i = jnp.arange(band)[None, :, None]
    k_i = jnp.arange(2 * band)[None, None, :] - band
    delta = q_i - k_i
    mask = (delta >= 0) & (delta <= band) & ((n_i > 0) | (k_i >= 0))
    s = jnp.where(mask, s, NEG)
    m = jnp.max(s, axis=-1, keepdims=True)
    p = jnp.exp(s - m)
    l = jnp.sum(p, axis=-1, keepdims=True)
    o = jnp.einsum('brhnqk,brhnkd->brhnqd', p, vs.astype(jnp.float32)) / l
    lse = (m + jnp.log(l))[..., 0]
    o = o.reshape(B, dilation, H, Lp, D)[:, :, :, :L].transpose(0, 3, 1, 2, 4).reshape(B, S, H, D)
    lse = lse.reshape(B, dilation, H, Lp)[:, :, :, :L].transpose(0, 3, 1, 2).reshape(B, S, H)
    return o, lse


def dilated_mixture(q, k, v):
    B, S, _, D = q.shape
    G = len(DIL_CONFIGS)
    qg = q.reshape(B, S, G, DIL_HEADS_PER_GROUP, D)
    kg = k.reshape(B, S, G, DIL_HEADS_PER_GROUP, D)
    vg = v.reshape(B, S, G, DIL_HEADS_PER_GROUP, D)
    outs, lses = [], []
    for g, (window, dilation) in enumerate(DIL_CONFIGS):
        o, lse = dilated_group(qg[:, :, g], kg[:, :, g], vg[:, :, g], window, dilation)
        outs.append(o)
        lses.append(lse)
    w = jax.nn.softmax(jnp.stack(lses, axis=0), axis=0)
    out = jnp.sum(w[..., None] * jnp.stack(outs, axis=0), axis=0)
    return out.astype(v.dtype)


def memory_attention(q, mk, mv):
    scale = q.shape[-1] ** -0.5
    s = jnp.einsum('bshd,bmhd->bhsm', q, mk, preferred_element_type=jnp.float32) * scale
    p = jax.nn.softmax(s, axis=-1).astype(mv.dtype)
    return jnp.einsum('bhsm,bmhd->bshd', p, mv)


def setup_inputs(seed: int = 0) -> dict:
    key = jax.random.key(seed)
    ks = jax.random.split(key, 11)
    f32 = jnp.float32
    x = jax.random.normal(ks[0], (BATCH, SEQ, D_MODEL), f32)
    mem = jax.random.normal(ks[1], (BATCH, MEM_LEN, D_MODEL), f32)
    norm_in_g = 1.0 + 0.05 * jax.random.normal(ks[2], (DEPTH, D_MODEL), f32)
    norm_mem_g = 1.0 + 0.05 * jax.random.normal(ks[3], (D_MODEL,), f32)
    w_in = jax.random.normal(ks[4], (DEPTH, D_MODEL, IN_WIDTH), f32) * D_MODEL ** -0.5
    w_mem_kv = jax.random.normal(ks[5], (DEPTH, D_MODEL, 2 * W_M), f32) * D_MODEL ** -0.5
    w_proj_a = jax.random.normal(ks[6], (DEPTH, W_A, D_MODEL), f32) * W_A ** -0.5
    w_proj_b = jax.random.normal(ks[7], (DEPTH, W_B_OUT, D_MODEL), f32) * W_B_OUT ** -0.5
    w_proj_m = jax.random.normal(ks[8], (DEPTH, W_M, D_MODEL), f32) * W_M ** -0.5
    w_out = jax.random.normal(ks[9], (DEPTH, D_MODEL, D_MODEL), f32) * D_MODEL ** -0.5
    norm_final_g = 1.0 + 0.05 * jax.random.normal(ks[10], (D_MODEL,), f32)
    return {"x": x, "mem": mem, "norm_in_g": norm_in_g, "norm_mem_g": norm_mem_g,
            "w_in": w_in, "w_mem_kv": w_mem_kv, "w_proj_a": w_proj_a, "w_proj_b": w_proj_b,
            "w_proj_m": w_proj_m, "w_out": w_out, "norm_final_g": norm_final_g}


def reference(x, mem, norm_in_g, norm_mem_g, w_in, w_mem_kv, w_proj_a, w_proj_b, w_proj_m,
              w_out, norm_final_g):
    B, S, _ = x.shape
    M = mem.shape[1]
    split_points = [int(p) for p in np.cumsum(IN_SPLITS)[:-1]]
    mem_n = rms_norm(mem, norm_mem_g)
    for layer in range(DEPTH):
        h = rms_norm(x, norm_in_g[layer])
        proj = jnp.einsum('bsd,de->bse', h, w_in[layer])
        (qa, ka, va, za, qb, kb, vb, zb, qm, zm, gates) = jnp.split(proj, split_points, axis=-1)

        qa = rope(qa.reshape(B, S, MOBA_HEADS, HEAD_DIM))
        ka = rope(ka.reshape(B, S, MOBA_HEADS, HEAD_DIM))
        va = va.reshape(B, S, MOBA_HEADS, HEAD_DIM)
        ya = moba_attention(qa, ka, va).reshape(B, S, W_A) * jax.nn.silu(za)

        hb = len(DIL_CONFIGS) * DIL_HEADS_PER_GROUP
        qb = rope(qb.reshape(B, S, hb, HEAD_DIM))
        kb = rope(kb.reshape(B, S, hb, HEAD_DIM))
        vb = vb.reshape(B, S, hb, HEAD_DIM)
        yb = dilated_mixture(qb, kb, vb).reshape(B, S, W_B_OUT) * jax.nn.silu(zb)

        mkv = jnp.einsum('bmd,de->bme', mem_n, w_mem_kv[layer])
        mk, mv = jnp.split(mkv, 2, axis=-1)
        ym = memory_attention(qm.reshape(B, S, MEM_HEADS, MEM_HEAD_DIM),
                              mk.reshape(B, M, MEM_HEADS, MEM_HEAD_DIM),
                              mv.reshape(B, M, MEM_HEADS, MEM_HEAD_DIM)).reshape(B, S, W_M)
        ym = ym * jax.nn.silu(zm)

        g_a, g_b, g_m = jnp.split(jax.nn.sigmoid(gates), N_BRANCH, axis=-1)
        merged = (g_a * jnp.einsum('bse,ed->bsd', ya, w_proj_a[layer])
                  + g_b * jnp.einsum('bse,ed->bsd', yb, w_proj_b[layer])
                  + g_m * jnp.einsum('bse,ed->bsd', ym, w_proj_m[layer]))
        x = x + jnp.einsum('bsd,de->bse', merged, w_out[layer])
    return rms_norm(x, norm_final_g)
```

```python
import functools

import jax
import jax.numpy as jnp
from jax import lax
from jax.experimental import pallas as pl
from jax.experimental.pallas import tpu as pltpu

F32 = jnp.float32
BF16 = jnp.bfloat16

HEAD_DIM = 128
MOBA_HEADS = 8
MOBA_BLOCK = 256
MOBA_TOPK = 3
DIL_CONFIGS = ((128, 1), (512, 4), (2048, 16))
DIL_HEADS_PER_GROUP = 4
MEM_HEADS = 4
MEM_HEAD_DIM = 256
ROPE_THETA = 10000.0
RMS_EPS = 1e-6
NEG = -1e30
N_BRANCH = 3

LANE = 128
V7X_VMEM_BYTES = 64 * 1024 * 1024
VMEM_LIMIT_BYTES = V7X_VMEM_BYTES - 8 * 1024 * 1024

IN_TM = 1024
IN_TN = 512
OUT_TM = 256

_CONTRACT_LAST = (((1,), (1,)), ((), ()))


def _segments(d_model):
    w_a = MOBA_HEADS * HEAD_DIM
    w_b = len(DIL_CONFIGS) * DIL_HEADS_PER_GROUP * HEAD_DIM
    w_bo = DIL_HEADS_PER_GROUP * HEAD_DIM
    w_m = MEM_HEADS * MEM_HEAD_DIM
    return (("qa", w_a, "rope"), ("ka", w_a, "rope"), ("va", w_a, "ident"), ("za", w_a, "silu"),
            ("qb", w_b, "rope"), ("kb", w_b, "rope"), ("vb", w_b, "ident"), ("zb", w_bo, "silu"),
            ("qm", w_m, "ident"), ("zm", w_m, "silu"), ("gates", N_BRANCH * d_model, "sigmoid"))


def _slot_offsets(d_model):
    offs, col = {}, 0
    for name, width, _ in _segments(d_model):
        offs[name] = col // LANE
        col += width
    return offs, col // LANE


def _rms_norm(xf, g):
    ms = jnp.mean(xf * xf, axis=-1, keepdims=True)
    return (xf * lax.rsqrt(ms + RMS_EPS)) * g


def _in_proj_kernel(x_ref, g_ref, w_ref, cos_ref, sin_ref, o_ref, h_ref, *, tile_kinds):
    j = pl.program_id(1)

    @pl.when(j == 0)
    def _():
        h_ref[...] = _rms_norm(x_ref[...], g_ref[...]).astype(BF16)

    acc = jnp.dot(h_ref[...], w_ref[...], preferred_element_type=F32)
    n_slots = o_ref.shape[0]

    def in_tiles(kind):
        pred = None
        for lo, hi in tile_kinds[kind]:
            p = (j >= lo) & (j < hi)
            pred = p if pred is None else (pred | p)
        return pred

    def store(fn):
        for s in range(n_slots):
            o_ref[s] = fn(acc[:, s * LANE:(s + 1) * LANE]).astype(BF16)

    @pl.when(in_tiles("rope"))
    def _():
        cos = cos_ref[...]
        sin = sin_ref[...]
        store(lambda a: a * cos + pltpu.roll(a, HEAD_DIM // 2, 1) * sin)

    @pl.when(in_tiles("ident"))
    def _():
        store(lambda a: a)

    @pl.when(in_tiles("silu"))
    def _():
        store(lambda a: a * jax.nn.sigmoid(a))

    @pl.when(in_tiles("sigmoid"))
    def _():
        store(jax.nn.sigmoid)


def _in_proj(x, g, w_bf, cos_t, sin_t):
    B, S, D = x.shape
    width = w_bf.shape[1]
    tm, tn = IN_TM, IN_TN
    assert S % tm == 0 and width % tn == 0 and tn % LANE == 0
    s_tiles = S // tm
    tile_kinds = {"rope": [], "ident": [], "silu": [], "sigmoid": []}
    col = 0
    for _, seg_w, kind in _segments(D):
        assert col % tn == 0 and seg_w % tn == 0
        tile_kinds[kind].append((col // tn, (col + seg_w) // tn))
        col += seg_w
    assert col == width
    return pl.pallas_call(
        functools.partial(_in_proj_kernel, tile_kinds=tile_kinds),
        grid=(B * s_tiles, width // tn),
        in_specs=[
            pl.BlockSpec((None, tm, D), lambda i, j: (i // s_tiles, i % s_tiles, 0)),
            pl.BlockSpec((1, D), lambda i, j: (0, 0)),
            pl.BlockSpec((D, tn), lambda i, j: (0, j)),
            pl.BlockSpec((tm, LANE), lambda i, j: (i % s_tiles, 0)),
            pl.BlockSpec((tm, LANE), lambda i, j: (i % s_tiles, 0)),
        ],
        out_specs=pl.BlockSpec((None, tn // LANE, tm, LANE),
                               lambda i, j: (i // s_tiles, j, i % s_tiles, 0)),
        out_shape=jax.ShapeDtypeStruct((B, width // LANE, S, LANE), BF16),
        scratch_shapes=[pltpu.VMEM((tm, D), BF16)],
        compiler_params=pltpu.CompilerParams(
            dimension_semantics=("arbitrary", "arbitrary"),
            vmem_limit_bytes=VMEM_LIMIT_BYTES),
        name="in_proj",
    )(x, g.reshape(1, D), w_bf, cos_t, sin_t)


def _moba_kernel(q_ref, k_ref, v_ref, z_ref, o_ref):
    S = q_ref.shape[0]
    blk = MOBA_BLOCK
    nb = S // blk
    scale = HEAD_DIM ** -0.5
    q = q_ref[...]
    k = k_ref[...]
    v = v_ref[...]

    kmean = jnp.mean(k.astype(F32).reshape(nb, blk, HEAD_DIM), axis=1)
    kmean = jnp.concatenate([kmean, jnp.zeros((LANE - nb, HEAD_DIM), F32)], axis=0)
    k_hi = kmean.astype(BF16)
    k_lo = (kmean - k_hi.astype(F32)).astype(BF16)
    gate = (lax.dot_general(q, k_hi, _CONTRACT_LAST, preferred_element_type=F32)
            + lax.dot_general(q, k_lo, _CONTRACT_LAST, preferred_element_type=F32))

    row = lax.broadcasted_iota(jnp.int32, (blk, blk), 0)
    col = lax.broadcasted_iota(jnp.int32, (blk, blk), 1)
    lane = lax.broadcasted_iota(jnp.int32, (blk, LANE), 1)

    for i in range(nb):
        qi = q[i * blk:(i + 1) * blk]
        kv_len = (i + 1) * blk
        s = lax.dot_general(qi, k[:kv_len], _CONTRACT_LAST, preferred_element_type=F32) * scale
        pieces = []
        if i > 0:
            g = gate[i * blk:(i + 1) * blk]
            rank = jnp.zeros((blk, LANE), jnp.int32)
            for jp in range(i):
                c = g[:, jp:jp + 1]
                beats = (c > g) | ((c == g) & (lane > jp))
                rank = rank + beats.astype(jnp.int32)
            sel = rank < MOBA_TOPK
            for j in range(i):
                pieces.append(jnp.where(sel[:, j:j + 1], s[:, j * blk:(j + 1) * blk], NEG))
        pieces.append(jnp.where(col <= row, s[:, i * blk:], NEG))
        s_all = jnp.concatenate(pieces, axis=1) if len(pieces) > 1 else pieces[0]
        m = jnp.max(s_all, axis=1, keepdims=True)
        p = jnp.exp(s_all - m)
        l = jnp.sum(p, axis=1, keepdims=True)
        pv = jnp.dot(p.astype(BF16), v[:kv_len], preferred_element_type=F32)
        out = (pv / l) * z_ref[i * blk:(i + 1) * blk, :].astype(F32)
        o_ref[i * blk:(i + 1) * blk, :] = out.astype(BF16)


def _moba(p_slots, offs):
    B, _, S, _ = p_slots.shape
    assert S % MOBA_BLOCK == 0 and S // MOBA_BLOCK <= LANE

    def slab(base):
        return pl.BlockSpec((None, None, S, LANE), lambda b, h: (b, base + h, 0, 0))

    return pl.pallas_call(
        _moba_kernel,
        grid=(B, MOBA_HEADS),
        in_specs=[slab(offs["qa"]), slab(offs["ka"]), slab(offs["va"]), slab(offs["za"])],
        out_specs=pl.BlockSpec((None, None, S, LANE), lambda b, h: (b, h, 0, 0)),
        out_shape=jax.ShapeDtypeStruct((B, MOBA_HEADS, S, LANE), BF16),
        compiler_params=pltpu.CompilerParams(
            dimension_semantics=("arbitrary", "arbitrary"),
            vmem_limit_bytes=VMEM_LIMIT_BYTES),
        name="moba",
    )(p_slots, p_slots, p_slots, p_slots)


def _dilated_kernel(q0, q1, q2, k0, k1, k2, v0, v1, v2, z_ref, o_ref, qs, ks, vs, o_scr, l_scr):
    S = z_ref.shape[0]
    scale = HEAD_DIM ** -0.5
    q_refs, k_refs, v_refs = (q0, q1, q2), (k0, k1, k2), (v0, v1, v2)

    for g, (window, d) in enumerate(DIL_CONFIGS):
        band = window // d
        nblk = (S // d) // band
        qs[...] = q_refs[g][...].astype(F32)
        ks[...] = k_refs[g][...].astype(F32)
        vs[...] = v_refs[g][...].astype(F32)
        row1 = lax.broadcasted_iota(jnp.int32, (band, band), 0)
        col1 = lax.broadcasted_iota(jnp.int32, (band, band), 1)
        row = lax.broadcasted_iota(jnp.int32, (band, 2 * band), 0)
        col = lax.broadcasted_iota(jnp.int32, (band, 2 * band), 1)
        mask_first = col1 <= row1
        mask_rest = (col >= row) & (col <= row + band)
        for r in range(d):
            k_prev = v_prev = None
            for n in range(nblk):
                start = r + n * band * d
                idx = pl.ds(start, band, stride=d) if d > 1 else pl.ds(start, band)
                qb = qs[idx, :].astype(BF16)
                kb = ks[idx, :].astype(BF16)
                vb = vs[idx, :].astype(BF16)
                if n == 0:
                    kk, vv, mask = kb, vb, mask_first
                else:
                    kk = jnp.concatenate([k_prev, kb], axis=0)
                    vv = jnp.concatenate([v_prev, vb], axis=0)
                    mask = mask_rest
                s = lax.dot_general(qb, kk, _CONTRACT_LAST, preferred_element_type=F32) * scale
                s = jnp.where(mask, s, NEG)
                m = jnp.max(s, axis=1, keepdims=True)
                p = jnp.exp(s - m)
                l = jnp.sum(p, axis=1, keepdims=True)
                o = jnp.dot(p.astype(BF16), vv, preferred_element_type=F32) / l
                o_scr[g, idx, :] = o
                l_scr[g, idx, :] = jnp.broadcast_to(m + jnp.log(l), (band, LANE))
                k_prev, v_prev = kb, vb

    lses = [l_scr[g] for g in range(len(DIL_CONFIGS))]
    m = functools.reduce(jnp.maximum, lses)
    es = [jnp.exp(l - m) for l in lses]
    den = functools.reduce(lambda a, b: a + b, es)
    num = functools.reduce(lambda a, b: a + b, [(e / den) * o_scr[g] for g, e in enumerate(es)])
    o_ref[...] = (num * z_ref[...].astype(F32)).astype(BF16)


def _dilated(p_slots, offs):
    B, _, S, _ = p_slots.shape
    G = len(DIL_CONFIGS)
    hg = DIL_HEADS_PER_GROUP
    for window, d in DIL_CONFIGS:
        band = window // d
        assert band == LANE and S % (d * band) == 0

    def slab(base):
        return pl.BlockSpec((None, None, S, LANE), lambda b, h: (b, base + h, 0, 0))

    in_specs = ([slab(offs["qb"] + g * hg) for g in range(G)]
                + [slab(offs["kb"] + g * hg) for g in range(G)]
                + [slab(offs["vb"] + g * hg) for g in range(G)]
                + [slab(offs["zb"])])
    return pl.pallas_call(
        _dilated_kernel,
        grid=(B, hg),
        in_specs=in_specs,
        out_specs=pl.BlockSpec((None, None, S, LANE), lambda b, h: (b, h, 0, 0)),
        out_shape=jax.ShapeDtypeStruct((B, hg, S, LANE), BF16),
        scratch_shapes=[pltpu.VMEM((S, LANE), F32)] * 3 + [pltpu.VMEM((G, S, LANE), F32)] * 2,
        compiler_params=pltpu.CompilerParams(
            dimension_semantics=("arbitrary", "arbitrary"),
            vmem_limit_bytes=VMEM_LIMIT_BYTES),
        name="dilated",
    )(*([p_slots] * (3 * G + 1)))


def _mem_kv_kernel(mem_ref, g_ref, w_ref, o_ref):
    h = _rms_norm(mem_ref[...], g_ref[...]).astype(BF16)
    o_ref[...] = jnp.dot(h, w_ref[...], preferred_element_type=F32).astype(BF16)


def _mem_kv(mem, g, w_bf):
    B, M, D = mem.shape
    width = w_bf.shape[1]
    return pl.pallas_call(
        _mem_kv_kernel,
        grid=(B,),
        in_specs=[
            pl.BlockSpec((None, M, D), lambda b: (b, 0, 0)),
            pl.BlockSpec((1, D), lambda b: (0, 0)),
            pl.BlockSpec((D, width), lambda b: (0, 0), pipeline_mode=pl.Buffered(1)),
        ],
        out_specs=pl.BlockSpec((None, M, width), lambda b: (b, 0, 0)),
        out_shape=jax.ShapeDtypeStruct((B, M, width), BF16),
        compiler_params=pltpu.CompilerParams(
            dimension_semantics=("arbitrary",),
            vmem_limit_bytes=VMEM_LIMIT_BYTES),
        name="mem_kv",
    )(mem, g.reshape(1, D), w_bf)


def _out_proj_kernel(x_ref, ya_ref, yb_ref, qm_ref, zm_ref, ga0, ga1, gb0, gb1, gm0, gm1, mkv_ref,
                     wa_ref, wb_ref, wm_ref, wo_ref, gf_ref, o_ref, *, final_norm):
    def slots(ref):
        return jnp.concatenate([ref[s] for s in range(ref.shape[0])], axis=1)

    w_m = MEM_HEADS * MEM_HEAD_DIM
    scale = MEM_HEAD_DIM ** -0.5
    qm = slots(qm_ref)
    zm = slots(zm_ref)
    ym = []
    for h in range(MEM_HEADS):
        c0, c1 = h * MEM_HEAD_DIM, (h + 1) * MEM_HEAD_DIM
        s = lax.dot_general(qm[:, c0:c1], mkv_ref[:, c0:c1], _CONTRACT_LAST,
                            preferred_element_type=F32) * scale
        m = jnp.max(s, axis=1, keepdims=True)
        p = jnp.exp(s - m)
        l = jnp.sum(p, axis=1, keepdims=True)
        pv = jnp.dot(p.astype(BF16), mkv_ref[:, w_m + c0:w_m + c1], preferred_element_type=F32)
        ym.append(((pv / l) * zm[:, c0:c1].astype(F32)).astype(BF16))
    ym = jnp.concatenate(ym, axis=1)

    def gate(r0, r1):
        return jnp.concatenate([slots(r0), slots(r1)], axis=1).astype(F32)

    merged = (gate(ga0, ga1) * jnp.dot(slots(ya_ref), wa_ref[...], preferred_element_type=F32)
              + gate(gb0, gb1) * jnp.dot(slots(yb_ref), wb_ref[...], preferred_element_type=F32)
              + gate(gm0, gm1) * jnp.dot(ym, wm_ref[...], preferred_element_type=F32))
    y = x_ref[...] + jnp.dot(merged.astype(BF16), wo_ref[...], preferred_element_type=F32)
    o_ref[...] = _rms_norm(y, gf_ref[...]) if final_norm else y


def _out_proj(x, ya, yb, p_slots, offs, mkv, wa, wb, wm, wo, g_final, final_norm):
    B, S, D = x.shape
    tm = OUT_TM
    gslots = D // LANE // 2
    assert S % tm == 0 and offs["gates"] % gslots == 0 and gslots == MOBA_HEADS
    assert offs["qm"] % gslots == 0 and offs["zm"] % gslots == 0

    def pblock(slot0):
        blk = slot0 // gslots
        return pl.BlockSpec((None, gslots, tm, LANE), lambda b, s: (b, blk, s, 0))

    def resident(w):
        return pl.BlockSpec(w.shape, lambda b, s: (0,) * w.ndim, pipeline_mode=pl.Buffered(1))

    gate_specs = [pblock(offs["gates"] + i * gslots) for i in range(2 * N_BRANCH)]
    return pl.pallas_call(
        functools.partial(_out_proj_kernel, final_norm=final_norm),
        grid=(B, S // tm),
        in_specs=[
            pl.BlockSpec((None, tm, D), lambda b, s: (b, s, 0)),
            pl.BlockSpec((None, MOBA_HEADS, tm, LANE), lambda b, s: (b, 0, s, 0)),
            pl.BlockSpec((None, DIL_HEADS_PER_GROUP, tm, LANE), lambda b, s: (b, 0, s, 0)),
            pblock(offs["qm"]), pblock(offs["zm"]), *gate_specs,
            pl.BlockSpec((None,) + mkv.shape[1:], lambda b, s: (b, 0, 0)),
            resident(wa), resident(wb), resident(wm), resident(wo),
            pl.BlockSpec((1, D), lambda b, s: (0, 0)),
        ],
        out_specs=pl.BlockSpec((None, tm, D), lambda b, s: (b, s, 0)),
        out_shape=jax.ShapeDtypeStruct((B, S, D), F32),
        compiler_params=pltpu.CompilerParams(
            dimension_semantics=("arbitrary", "arbitrary"),
            vmem_limit_bytes=VMEM_LIMIT_BYTES),
        name="out_proj",
    )(x, ya, yb, *([p_slots] * (2 + 2 * N_BRANCH)), mkv, wa, wb, wm, wo, g_final.reshape(1, D))


def _rope_tables(S):
    half = HEAD_DIM // 2
    inv = ROPE_THETA ** (-jnp.arange(half, dtype=F32) / half)
    ang = jnp.arange(S, dtype=F32)[:, None] * inv[None, :]
    cos, sin = jnp.cos(ang), jnp.sin(ang)
    return jnp.concatenate([cos, cos], axis=1), jnp.concatenate([-sin, sin], axis=1)


def kernel(x, mem, norm_in_g, norm_mem_g, w_in, w_mem_kv, w_proj_a, w_proj_b, w_proj_m, w_out, norm_final_g):
    B, S, D = x.shape
    depth = w_in.shape[0]
    offs, n_slots = _slot_offsets(D)
    assert w_in.shape[2] == n_slots * LANE
    cos_t, sin_t = _rope_tables(S)
    for layer in range(depth):
        p_slots = _in_proj(x, norm_in_g[layer], w_in[layer].astype(BF16), cos_t, sin_t)
        ya = _moba(p_slots, offs)
        yb = _dilated(p_slots, offs)
        mkv = _mem_kv(mem, norm_mem_g, w_mem_kv[layer].astype(BF16))
        x = _out_proj(x, ya, yb, p_slots, offs, mkv,
                      w_proj_a[layer].astype(BF16), w_proj_b[layer].astype(BF16),
                      w_proj_m[layer].astype(BF16), w_out[layer].astype(BF16),
                      norm_final_g, final_norm=(layer == depth - 1))
    return x
```

```python
import functools

import jax
import jax.numpy as jnp
from jax import lax
from jax.experimental import pallas as pl
from jax.experimental.pallas import tpu as pltpu

F32 = jnp.float32
BF16 = jnp.bfloat16

HEAD_DIM = 128
MOBA_HEADS = 8
MOBA_BLOCK = 256
MOBA_TOPK = 3
DIL_CONFIGS = ((128, 1), (512, 4), (2048, 16))
DIL_HEADS_PER_GROUP = 4
MEM_HEADS = 4
MEM_HEAD_DIM = 256
ROPE_THETA = 10000.0
RMS_EPS = 1e-6
NEG = -1e30
N_BRANCH = 3

LANE = 128
V7X_VMEM_BYTES = 64 * 1024 * 1024
VMEM_LIMIT_BYTES = V7X_VMEM_BYTES - 8 * 1024 * 1024

IN_TM = 1024
IN_TN = 512
IN_CM = 256
OUT_TM = 256

_CONTRACT_LAST = (((1,), (1,)), ((), ()))
_LOG2E = 1.4426950408889634


def _segments(d_model):
    w_a = MOBA_HEADS * HEAD_DIM
    w_b = len(DIL_CONFIGS) * DIL_HEADS_PER_GROUP * HEAD_DIM
    w_bo = DIL_HEADS_PER_GROUP * HEAD_DIM
    w_m = MEM_HEADS * MEM_HEAD_DIM
    return (("qa", w_a, "rope"), ("ka", w_a, "rope"), ("va", w_a, "ident"), ("za", w_a, "silu"),
            ("qb", w_b, "rope"), ("kb", w_b, "rope"), ("vb", w_b, "ident"), ("zb", w_bo, "silu"),
            ("qm", w_m, "ident"), ("zm", w_m, "silu"), ("gates", N_BRANCH * d_model, "sigmoid"))


def _slot_offsets(d_model):
    offs, col = {}, 0
    for name, width, _ in _segments(d_model):
        offs[name] = col // LANE
        col += width
    return offs, col // LANE


def _rms_norm(xf, g):
    ms = jnp.mean(xf * xf, axis=-1, keepdims=True)
    return (xf * lax.rsqrt(ms + RMS_EPS)) * g


def _in_proj_kernel(x_ref, g_ref, w_ref, cos_ref, sin_ref, o_ref, h_ref, *, tile_kinds):
    j = pl.program_id(1)

    @pl.when(j == 0)
    def _():
        h_ref[...] = _rms_norm(x_ref[...], g_ref[...]).astype(BF16)

    n_slots = o_ref.shape[0]
    tm = h_ref.shape[0]

    def in_tiles(kind):
        pred = None
        for lo, hi in tile_kinds[kind]:
            p = (j >= lo) & (j < hi)
            pred = p if pred is None else (pred | p)
        return pred

    def run(epilogue):
        w = w_ref[...].astype(BF16)
        for c in range(tm // IN_CM):
            rows = pl.ds(c * IN_CM, IN_CM)
            acc = jnp.dot(h_ref[rows, :], w, preferred_element_type=F32)
            for s in range(n_slots):
                o_ref[s, rows, :] = epilogue(acc[:, s * LANE:(s + 1) * LANE], rows).astype(BF16)

    @pl.when(in_tiles("rope"))
    def _():
        run(lambda a, rows: a * cos_ref[rows, :] + pltpu.roll(a, HEAD_DIM // 2, 1) * sin_ref[rows, :])

    @pl.when(in_tiles("ident"))
    def _():
        run(lambda a, rows: a)

    @pl.when(in_tiles("silu"))
    def _():
        run(lambda a, rows: a * jax.nn.sigmoid(a))

    @pl.when(in_tiles("sigmoid"))
    def _():
        run(lambda a, rows: jax.nn.sigmoid(a))


def _in_proj(x, g, w, cos_t, sin_t):
    B, S, D = x.shape
    width = w.shape[1]
    tm, tn = IN_TM, IN_TN
    assert S % tm == 0 and width % tn == 0 and tn % LANE == 0 and tm % IN_CM == 0
    s_tiles = S // tm
    tile_kinds = {"rope": [], "ident": [], "silu": [], "sigmoid": []}
    col = 0
    for _, seg_w, kind in _segments(D):
        assert col % tn == 0 and seg_w % tn == 0
        tile_kinds[kind].append((col // tn, (col + seg_w) // tn))
        col += seg_w
    assert col == width
    return pl.pallas_call(
        functools.partial(_in_proj_kernel, tile_kinds=tile_kinds),
        grid=(B * s_tiles, width // tn),
        in_specs=[
            pl.BlockSpec((None, tm, D), lambda i, j: (i // s_tiles, i % s_tiles, 0)),
            pl.BlockSpec((1, D), lambda i, j: (0, 0)),
            pl.BlockSpec((D, tn), lambda i, j: (0, j)),
            pl.BlockSpec((tm, LANE), lambda i, j: (i % s_tiles, 0)),
            pl.BlockSpec((tm, LANE), lambda i, j: (i % s_tiles, 0)),
        ],
        out_specs=pl.BlockSpec((None, tn // LANE, tm, LANE),
                               lambda i, j: (i // s_tiles, j, i % s_tiles, 0)),
        out_shape=jax.ShapeDtypeStruct((B, width // LANE, S, LANE), BF16),
        scratch_shapes=[pltpu.VMEM((tm, D), BF16)],
        compiler_params=pltpu.CompilerParams(
            dimension_semantics=("arbitrary", "arbitrary"),
            vmem_limit_bytes=VMEM_LIMIT_BYTES),
        name="in_proj",
    )(x, g.reshape(1, D), w, cos_t, sin_t)


def _moba_kernel(q_ref, k_ref, v_ref, z_ref, o_ref):
    S = q_ref.shape[0]
    blk = MOBA_BLOCK
    nb = S // blk
    scale = HEAD_DIM ** -0.5
    q = q_ref[...]
    k = k_ref[...]
    v = v_ref[...]

    kmean = jnp.mean(k.astype(F32).reshape(nb, blk, HEAD_DIM), axis=1)
    kmean = jnp.concatenate([kmean, jnp.zeros((LANE - nb, HEAD_DIM), F32)], axis=0)
    k_hi = kmean.astype(BF16)
    k_lo = (kmean - k_hi.astype(F32)).astype(BF16)
    gate = (lax.dot_general(q, k_hi, _CONTRACT_LAST, preferred_element_type=F32)
            + lax.dot_general(q, k_lo, _CONTRACT_LAST, preferred_element_type=F32))

    row = lax.broadcasted_iota(jnp.int32, (blk, blk), 0)
    col = lax.broadcasted_iota(jnp.int32, (blk, blk), 1)
    lane = lax.broadcasted_iota(jnp.int32, (blk, LANE), 1)

    key_blk = lax.broadcasted_iota(jnp.int32, (S, LANE), 0) // blk
    onehot = (key_blk == lax.broadcasted_iota(jnp.int32, (S, LANE), 1)).astype(BF16)
    k_aug = jnp.concatenate([k, onehot], axis=1)

    for i in range(nb):
        qi = q[i * blk:(i + 1) * blk]
        kv_len = (i + 1) * blk
        if i > 0:
            g = gate[i * blk:(i + 1) * blk]
            rank = jnp.zeros((blk, LANE), jnp.int32)
            for jp in range(i):
                c = g[:, jp:jp + 1]
                beats = (c > g) | ((c == g) & (lane > jp))
                rank = rank + beats.astype(jnp.int32)
            bias = jnp.where((rank >= MOBA_TOPK) & (lane < i), NEG, 0.0).astype(BF16)
            q_aug = jnp.concatenate([qi, bias], axis=1)
            s = lax.dot_general(q_aug, k_aug[:kv_len], _CONTRACT_LAST, preferred_element_type=F32)
            own = jnp.where(col <= row, s[:, i * blk:], NEG)
            s_all = jnp.concatenate([s[:, :i * blk], own], axis=1)
        else:
            s = lax.dot_general(qi, k[:kv_len], _CONTRACT_LAST, preferred_element_type=F32)
            s_all = jnp.where(col <= row, s, NEG)
        m = jnp.max(s_all, axis=1, keepdims=True)
        p = jnp.exp2((s_all - m) * (scale * _LOG2E))
        l = jnp.sum(p, axis=1, keepdims=True)
        pv = jnp.dot(p.astype(BF16), v[:kv_len], preferred_element_type=F32)
        out = (pv / l) * z_ref[i * blk:(i + 1) * blk, :].astype(F32)
        o_ref[i * blk:(i + 1) * blk, :] = out.astype(BF16)


def _moba(p_slots, offs):
    B, _, S, _ = p_slots.shape
    assert S % MOBA_BLOCK == 0 and S // MOBA_BLOCK <= LANE

    def slab(base):
        return pl.BlockSpec((None, None, S, LANE), lambda b, h: (b, base + h, 0, 0))

    return pl.pallas_call(
        _moba_kernel,
        grid=(B, MOBA_HEADS),
        in_specs=[slab(offs["qa"]), slab(offs["ka"]), slab(offs["va"]), slab(offs["za"])],
        out_specs=pl.BlockSpec((None, None, S, LANE), lambda b, h: (b, h, 0, 0)),
        out_shape=jax.ShapeDtypeStruct((B, MOBA_HEADS, S, LANE), BF16),
        compiler_params=pltpu.CompilerParams(
            dimension_semantics=("arbitrary", "arbitrary"),
            vmem_limit_bytes=VMEM_LIMIT_BYTES),
        name="moba",
    )(p_slots, p_slots, p_slots, p_slots)


def _dilated_kernel(q0, q1, q2, k0, k1, k2, v0, v1, v2, z_ref, o_ref, qs, ks, vs, o_scr, l_scr):
    S = z_ref.shape[0]
    scale = HEAD_DIM ** -0.5
    q_refs, k_refs, v_refs = (q0, q1, q2), (k0, k1, k2), (v0, v1, v2)

    for g, (window, d) in enumerate(DIL_CONFIGS):
        band = window // d
        nblk = (S // d) // band
        qs[...] = q_refs[g][...].astype(F32)
        ks[...] = k_refs[g][...].astype(F32)
        vs[...] = v_refs[g][...].astype(F32)
        row1 = lax.broadcasted_iota(jnp.int32, (band, band), 0)
        col1 = lax.broadcasted_iota(jnp.int32, (band, band), 1)
        row = lax.broadcasted_iota(jnp.int32, (band, 2 * band), 0)
        col = lax.broadcasted_iota(jnp.int32, (band, 2 * band), 1)
        mask_first = col1 <= row1
        mask_rest = (col >= row) & (col <= row + band)
        for r in range(d):
            k_prev = v_prev = None
            for n in range(nblk):
                start = r + n * band * d
                idx = pl.ds(start, band, stride=d) if d > 1 else pl.ds(start, band)
                qb = qs[idx, :].astype(BF16)
                kb = ks[idx, :].astype(BF16)
                vb = vs[idx, :].astype(BF16)
                if n == 0:
                    kk, vv, mask = kb, vb, mask_first
                else:
                    kk = jnp.concatenate([k_prev, kb], axis=0)
                    vv = jnp.concatenate([v_prev, vb], axis=0)
                    mask = mask_rest
                s = lax.dot_general(qb, kk, _CONTRACT_LAST, preferred_element_type=F32) * scale
                s = jnp.where(mask, s, NEG)
                m = jnp.max(s, axis=1, keepdims=True)
                p = jnp.exp(s - m)
                l = jnp.sum(p, axis=1, keepdims=True)
                o = jnp.dot(p.astype(BF16), vv, preferred_element_type=F32) / l
                o_scr[g, idx, :] = o
                l_scr[g, idx, :] = jnp.broadcast_to(m + jnp.log(l), (band, LANE))
                k_prev, v_prev = kb, vb

    lses = [l_scr[g] for g in range(len(DIL_CONFIGS))]
    m = functools.reduce(jnp.maximum, lses)
    es = [jnp.exp(l - m) for l in lses]
    den = functools.reduce(lambda a, b: a + b, es)
    num = functools.reduce(lambda a, b: a + b, [(e / den) * o_scr[g] for g, e in enumerate(es)])
    o_ref[...] = (num * z_ref[...].astype(F32)).astype(BF16)


def _dilated(p_slots, offs):
    B, _, S, _ = p_slots.shape
    G = len(DIL_CONFIGS)
    hg = DIL_HEADS_PER_GROUP
    for window, d in DIL_CONFIGS:
        band = window // d
        assert band == LANE and S % (d * band) == 0

    def slab(base):
        return pl.BlockSpec((None, None, S, LANE), lambda b, h: (b, base + h, 0, 0))

    in_specs = ([slab(offs["qb"] + g * hg) for g in range(G)]
                + [slab(offs["kb"] + g * hg) for g in range(G)]
                + [slab(offs["vb"] + g * hg) for g in range(G)]
                + [slab(offs["zb"])])
    return pl.pallas_call(
        _dilated_kernel,
        grid=(B, hg),
        in_specs=in_specs,
        out_specs=pl.BlockSpec((None, None, S, LANE), lambda b, h: (b, h, 0, 0)),
        out_shape=jax.ShapeDtypeStruct((B, hg, S, LANE), BF16),
        scratch_shapes=[pltpu.VMEM((S, LANE), F32)] * 3 + [pltpu.VMEM((G, S, LANE), F32)] * 2,
        compiler_params=pltpu.CompilerParams(
            dimension_semantics=("arbitrary", "arbitrary"),
            vmem_limit_bytes=VMEM_LIMIT_BYTES),
        name="dilated",
    )(*([p_slots] * (3 * G + 1)))


def _mem_kv_kernel(mem_ref, g_ref, w_ref, o_ref):
    h = _rms_norm(mem_ref[...], g_ref[...]).astype(BF16)
    o_ref[...] = jnp.dot(h, w_ref[...], preferred_element_type=F32).astype(BF16)


def _mem_kv(mem, g, w_bf):
    B, M, D = mem.shape
    width = w_bf.shape[1]
    return pl.pallas_call(
        _mem_kv_kernel,
        grid=(B,),
        in_specs=[
            pl.BlockSpec((None, M, D), lambda b: (b, 0, 0)),
            pl.BlockSpec((1, D), lambda b: (0, 0)),
            pl.BlockSpec((D, width), lambda b: (0, 0), pipeline_mode=pl.Buffered(1)),
        ],
        out_specs=pl.BlockSpec((None, M, width), lambda b: (b, 0, 0)),
        out_shape=jax.ShapeDtypeStruct((B, M, width), BF16),
        compiler_params=pltpu.CompilerParams(
            dimension_semantics=("arbitrary",),
            vmem_limit_bytes=VMEM_LIMIT_BYTES),
        name="mem_kv",
    )(mem, g.reshape(1, D), w_bf)


def _out_proj_kernel(x_ref, ya_ref, yb_ref, qm_ref, zm_ref, ga0, ga1, gb0, gb1, gm0, gm1, mkv_ref,
                     wa_ref, wb_ref, wm_ref, wo_ref, gf_ref, o_ref, *, final_norm):
    def slots(ref):
        return jnp.concatenate([ref[s] for s in range(ref.shape[0])], axis=1)

    w_m = MEM_HEADS * MEM_HEAD_DIM
    scale = MEM_HEAD_DIM ** -0.5
    qm = slots(qm_ref)
    zm = slots(zm_ref)
    ym = []
    for h in range(MEM_HEADS):
        c0, c1 = h * MEM_HEAD_DIM, (h + 1) * MEM_HEAD_DIM
        s = lax.dot_general(qm[:, c0:c1], mkv_ref[:, c0:c1], _CONTRACT_LAST,
                            preferred_element_type=F32) * scale
        m = jnp.max(s, axis=1, keepdims=True)
        p = jnp.exp(s - m)
        l = jnp.sum(p, axis=1, keepdims=True)
        pv = jnp.dot(p.astype(BF16), mkv_ref[:, w_m + c0:w_m + c1], preferred_element_type=F32)
        ym.append(((pv / l) * zm[:, c0:c1].astype(F32)).astype(BF16))
    ym = jnp.concatenate(ym, axis=1)

    def gate(r0, r1):
        return jnp.concatenate([slots(r0), slots(r1)], axis=1).astype(F32)

    merged = (gate(ga0, ga1) * jnp.dot(slots(ya_ref), wa_ref[...], preferred_element_type=F32)
              + gate(gb0, gb1) * jnp.dot(slots(yb_ref), wb_ref[...], preferred_element_type=F32)
              + gate(gm0, gm1) * jnp.dot(ym, wm_ref[...], preferred_element_type=F32))
    y = x_ref[...] + jnp.dot(merged.astype(BF16), wo_ref[...], preferred_element_type=F32)
    o_ref[...] = _rms_norm(y, gf_ref[...]) if final_norm else y


def _out_proj(x, ya, yb, p_slots, offs, mkv, wa, wb, wm, wo, g_final, final_norm):
    B, S, D = x.shape
    tm = OUT_TM
    gslots = D // LANE // 2
    assert S % tm == 0 and offs["gates"] % gslots == 0 and gslots == MOBA_HEADS
    assert offs["qm"] % gslots == 0 and offs["zm"] % gslots == 0

    def pblock(slot0):
        blk = slot0 // gslots
        return pl.BlockSpec((None, gslots, tm, LANE), lambda b, s: (b, blk, s, 0))

    def resident(w):
        return pl.BlockSpec(w.shape, lambda b, s: (0,) * w.ndim, pipeline_mode=pl.Buffered(1))

    gate_specs = [pblock(offs["gates"] + i * gslots) for i in range(2 * N_BRANCH)]
    return pl.pallas_call(
        functools.partial(_out_proj_kernel, final_norm=final_norm),
        grid=(B, S // tm),
        in_specs=[
            pl.BlockSpec((None, tm, D), lambda b, s: (b, s, 0)),
            pl.BlockSpec((None, MOBA_HEADS, tm, LANE), lambda b, s: (b, 0, s, 0)),
            pl.BlockSpec((None, DIL_HEADS_PER_GROUP, tm, LANE), lambda b, s: (b, 0, s, 0)),
            pblock(offs["qm"]), pblock(offs["zm"]), *gate_specs,
            pl.BlockSpec((None,) + mkv.shape[1:], lambda b, s: (b, 0, 0)),
            resident(wa), resident(wb), resident(wm), resident(wo),
            pl.BlockSpec((1, D), lambda b, s: (0, 0)),
        ],
        out_specs=pl.BlockSpec((None, tm, D), lambda b, s: (b, s, 0)),
        out_shape=jax.ShapeDtypeStruct((B, S, D), F32),
        compiler_params=pltpu.CompilerParams(
            dimension_semantics=("arbitrary", "arbitrary"),
            vmem_limit_bytes=VMEM_LIMIT_BYTES),
        name="out_proj",
    )(x, ya, yb, *([p_slots] * (2 + 2 * N_BRANCH)), mkv, wa, wb, wm, wo, g_final.reshape(1, D))


def _rope_tables(S):
    half = HEAD_DIM // 2
    inv = ROPE_THETA ** (-jnp.arange(half, dtype=F32) / half)
    ang = jnp.arange(S, dtype=F32)[:, None] * inv[None, :]
    cos, sin = jnp.cos(ang), jnp.sin(ang)
    return jnp.concatenate([cos, cos], axis=1), jnp.concatenate([-sin, sin], axis=1)


def kernel(x, mem, norm_in_g, norm_mem_g, w_in, w_mem_kv, w_proj_a, w_proj_b, w_proj_m, w_out, norm_final_g):
    B, S, D = x.shape
    depth = w_in.shape[0]
    offs, n_slots = _slot_offsets(D)
    assert w_in.shape[2] == n_slots * LANE
    cos_t, sin_t = _rope_tables(S)
    for layer in range(depth):
        p_slots = _in_proj(x, norm_in_g[layer], w_in[layer], cos_t, sin_t)
        ya = _moba(p_slots, offs)
        yb = _dilated(p_slots, offs)
        mkv = _mem_kv(mem, norm_mem_g, w_mem_kv[layer].astype(BF16))
        x = _out_proj(x, ya, yb, p_slots, offs, mkv,
                      w_proj_a[layer].astype(BF16), w_proj_b[layer].astype(BF16),
                      w_proj_m[layer].astype(BF16), w_out[layer].astype(BF16),
                      norm_final_g, final_norm=(layer == depth - 1))
    return x
```

```python
import functools

import jax
import jax.numpy as jnp
from jax import lax
from jax.experimental import pallas as pl
from jax.experimental.pallas import tpu as pltpu

F32 = jnp.float32
BF16 = jnp.bfloat16

HEAD_DIM = 128
MOBA_HEADS = 8
MOBA_BLOCK = 256
MOBA_TOPK = 3
DIL_CONFIGS = ((128, 1), (512, 4), (2048, 16))
DIL_HEADS_PER_GROUP = 4
MEM_HEADS = 4
MEM_HEAD_DIM = 256
ROPE_THETA = 10000.0
RMS_EPS = 1e-6
NEG = -1e30
N_BRANCH = 3

LANE = 128
BF16_SUBLANES = 16
V7X_VMEM_BYTES = 64 * 1024 * 1024
VMEM_LIMIT_BYTES = V7X_VMEM_BYTES - 8 * 1024 * 1024

IN_TN = 512
IN_CM = 256
IN_STAGE_BUFS = 8
OUT_TM = 256

_CONTRACT_LAST = (((1,), (1,)), ((), ()))
_LOG2E = 1.4426950408889634


def _segments(d_model):
    w_a = MOBA_HEADS * HEAD_DIM
    w_g = DIL_HEADS_PER_GROUP * HEAD_DIM
    w_m = MEM_HEADS * MEM_HEAD_DIM

    def groups(name, kind):
        return tuple((f"{name}{g}", w_g, kind, d) for g, (_, d) in enumerate(DIL_CONFIGS))

    return (("qa", w_a, "rope", 1), ("ka", w_a, "rope", 1), ("va", w_a, "ident", 1), ("za", w_a, "silu", 1),
            *groups("qb", "rope"), *groups("kb", "rope"), *groups("vb", "ident"), ("zb", w_g, "silu", 1),
            ("qm", w_m, "ident", 1), ("zm", w_m, "silu", 1), ("gates", N_BRANCH * d_model, "sigmoid", 1))


def _slot_offsets(d_model):
    offs, col = {}, 0
    for name, width, _, _ in _segments(d_model):
        offs[name] = col // LANE
        col += width
    return offs, col // LANE


def _rms_norm(xf, g):
    ms = jnp.mean(xf * xf, axis=-1, keepdims=True)
    return (xf * lax.rsqrt(ms + RMS_EPS)) * g


def _in_proj_kernel(x_hbm, g_ref, w_ref, cos_ref, sin_ref, o_ref, x_buf, h_ref, stage_ref, x_sem, *,
                    tile_kinds):
    i = pl.program_id(0)
    j = pl.program_id(1)
    n_slots = o_ref.shape[0]
    S = h_ref.shape[0]

    def x_copy(b):
        return pltpu.make_async_copy(x_hbm.at[b], x_buf, x_sem)

    @pl.when((i == 0) & (j == 0))
    def _():
        x_copy(0).start()

    @pl.when(j == 0)
    def _():
        x_copy(i).wait()
        h_ref[...] = _rms_norm(x_buf[...], g_ref[...]).astype(BF16)

    @pl.when((j == 1) & (i + 1 < pl.num_programs(0)))
    def _():
        x_copy(i + 1).start()

    def in_tiles(ranges):
        pred = None
        for lo, hi in ranges:
            p = (j >= lo) & (j < hi)
            pred = p if pred is None else (pred | p)
        return pred

    def run(epilogue, d):
        w = w_ref[...].astype(BF16)
        n = IN_CM // d
        for c in range(S // IN_CM):
            rows = pl.ds(c * IN_CM, IN_CM)
            acc = jnp.dot(h_ref[rows, :], w, preferred_element_type=F32)
            for s in range(n_slots):
                e = epilogue(acc[:, s * LANE:(s + 1) * LANE], rows)
                if d == 1:
                    o_ref[s, rows, :] = e.astype(BF16)
                    continue
                stage = stage_ref.at[(c * n_slots + s) % stage_ref.shape[0]]
                stage[...] = e
                for r in range(d):
                    o_ref[s, pl.ds(r * (S // d) + c * n, n), :] = stage[pl.ds(r, n, stride=d), :].astype(BF16)

    epilogues = {
        "rope": lambda a, rows: a * cos_ref[rows, :] + pltpu.roll(a, HEAD_DIM // 2, 1) * sin_ref[rows, :],
        "ident": lambda a, rows: a,
        "silu": lambda a, rows: a * jax.nn.sigmoid(a),
        "sigmoid": lambda a, rows: jax.nn.sigmoid(a),
    }
    for (kind, d), ranges in tile_kinds.items():
        pl.when(in_tiles(ranges))(functools.partial(run, epilogues[kind], d))


def _in_proj(x, g, w, cos_t, sin_t):
    B, S, D = x.shape
    width = w.shape[1]
    tn = IN_TN
    assert S % IN_CM == 0 and width % tn == 0 and tn % LANE == 0
    tile_kinds = {}
    col = 0
    for _, seg_w, kind, d in _segments(D):
        assert col % tn == 0 and seg_w % tn == 0 and (IN_CM // d) % BF16_SUBLANES == 0
        tile_kinds.setdefault((kind, d), []).append((col // tn, (col + seg_w) // tn))
        col += seg_w
    assert col == width

    def table_spec():
        return pl.BlockSpec((S, LANE), lambda i, j: (0, 0), pipeline_mode=pl.Buffered(1))

    return pl.pallas_call(
        functools.partial(_in_proj_kernel, tile_kinds=tile_kinds),
        grid=(B, width // tn),
        in_specs=[
            pl.BlockSpec(memory_space=pltpu.HBM),
            pl.BlockSpec((1, D), lambda i, j: (0, 0)),
            pl.BlockSpec((D, tn), lambda i, j: (0, j)),
            table_spec(), table_spec(),
        ],
        out_specs=pl.BlockSpec((None, tn // LANE, S, LANE), lambda i, j: (i, j, 0, 0)),
        out_shape=jax.ShapeDtypeStruct((B, width // LANE, S, LANE), BF16),
        scratch_shapes=[pltpu.VMEM((S, D), F32), pltpu.VMEM((S, D), BF16),
                        pltpu.VMEM((IN_STAGE_BUFS, IN_CM, LANE), F32), pltpu.SemaphoreType.DMA(())],
        compiler_params=pltpu.CompilerParams(
            dimension_semantics=("arbitrary", "arbitrary"),
            vmem_limit_bytes=VMEM_LIMIT_BYTES),
        name="in_proj",
    )(x, g.reshape(1, D), w, cos_t, sin_t)


def _moba_kernel(q_ref, k_ref, v_ref, z_ref, o_ref):
    S = q_ref.shape[0]
    blk = MOBA_BLOCK
    nb = S // blk
    scale = HEAD_DIM ** -0.5
    q = q_ref[...]
    k = k_ref[...]
    v = v_ref[...]

    kmean = jnp.mean(k.astype(F32).reshape(nb, blk, HEAD_DIM), axis=1)
    kmean = jnp.concatenate([kmean, jnp.zeros((LANE - nb, HEAD_DIM), F32)], axis=0)
    k_hi = kmean.astype(BF16)
    k_lo = (kmean - k_hi.astype(F32)).astype(BF16)
    gate = (lax.dot_general(q, k_hi, _CONTRACT_LAST, preferred_element_type=F32)
            + lax.dot_general(q, k_lo, _CONTRACT_LAST, preferred_element_type=F32))

    row = lax.broadcasted_iota(jnp.int32, (blk, blk), 0)
    col = lax.broadcasted_iota(jnp.int32, (blk, blk), 1)
    lane = lax.broadcasted_iota(jnp.int32, (blk, LANE), 1)

    key_blk = lax.broadcasted_iota(jnp.int32, (S, LANE), 0) // blk
    onehot = (key_blk == lax.broadcasted_iota(jnp.int32, (S, LANE), 1)).astype(BF16)
    k_aug = jnp.concatenate([k, onehot], axis=1)

    for i in range(nb):
        qi = q[i * blk:(i + 1) * blk]
        kv_len = (i + 1) * blk
        if i > 0:
            g = gate[i * blk:(i + 1) * blk]
            rank = jnp.zeros((blk, LANE), jnp.int32)
            for jp in range(i):
                c = g[:, jp:jp + 1]
                beats = (c > g) | ((c == g) & (lane > jp))
                rank = rank + beats.astype(jnp.int32)
            bias = jnp.where((rank >= MOBA_TOPK) & (lane < i), NEG, 0.0).astype(BF16)
            q_aug = jnp.concatenate([qi, bias], axis=1)
            s = lax.dot_general(q_aug, k_aug[:kv_len], _CONTRACT_LAST, preferred_element_type=F32)
            own = jnp.where(col <= row, s[:, i * blk:], NEG)
            s_all = jnp.concatenate([s[:, :i * blk], own], axis=1)
        else:
            s = lax.dot_general(qi, k[:kv_len], _CONTRACT_LAST, preferred_element_type=F32)
            s_all = jnp.where(col <= row, s, NEG)
        m = jnp.max(s_all, axis=1, keepdims=True)
        p = jnp.exp2((s_all - m) * (scale * _LOG2E))
        l = jnp.sum(p, axis=1, keepdims=True)
        pv = jnp.dot(p.astype(BF16), v[:kv_len], preferred_element_type=F32)
        out = (pv / l) * z_ref[i * blk:(i + 1) * blk, :].astype(F32)
        o_ref[i * blk:(i + 1) * blk, :] = out.astype(BF16)


def _moba(p_slots, offs):
    B, _, S, _ = p_slots.shape
    assert S % MOBA_BLOCK == 0 and S // MOBA_BLOCK <= LANE

    def slab(base):
        return pl.BlockSpec((None, None, S, LANE), lambda b, h: (b, base + h, 0, 0))

    return pl.pallas_call(
        _moba_kernel,
        grid=(B, MOBA_HEADS),
        in_specs=[slab(offs["qa"]), slab(offs["ka"]), slab(offs["va"]), slab(offs["za"])],
        out_specs=pl.BlockSpec((None, None, S, LANE), lambda b, h: (b, h, 0, 0)),
        out_shape=jax.ShapeDtypeStruct((B, MOBA_HEADS, S, LANE), BF16),
        compiler_params=pltpu.CompilerParams(
            dimension_semantics=("arbitrary", "arbitrary"),
            vmem_limit_bytes=VMEM_LIMIT_BYTES),
        name="moba",
    )(p_slots, p_slots, p_slots, p_slots)


def _dilated_kernel(q0, q1, q2, k0, k1, k2, v0, v1, v2, z_ref, o_ref, o_scr, l_scr):
    S = z_ref.shape[0]
    scale = HEAD_DIM ** -0.5
    q_refs, k_refs, v_refs = (q0, q1, q2), (k0, k1, k2), (v0, v1, v2)

    band = LANE
    n_blocks = S // band
    row = lax.broadcasted_iota(jnp.int32, (band, 2 * band), 0)
    col = lax.broadcasted_iota(jnp.int32, (band, 2 * band), 1)
    mask_rest = (col >= row) & (col <= row + band)
    mask_first = mask_rest & (col >= band)

    for g, (window, d) in enumerate(DIL_CONFIGS):
        assert window // d == band
        nblk = S // d // band
        q = q_refs[g][...]
        k = k_refs[g][...]
        v = v_refs[g][...]

        def rows(b):
            return slice(b * band, (b + 1) * band)

        def window_rows(t, b):
            if b % nblk == 0:
                return jnp.concatenate([t[rows(b)], t[rows(b)]], axis=0)
            return t[(b - 1) * band:(b + 1) * band]

        s = jnp.concatenate(
            [jnp.where(mask_first if b % nblk == 0 else mask_rest,
                       lax.dot_general(q[rows(b)], window_rows(k, b), _CONTRACT_LAST,
                                       preferred_element_type=F32), NEG)
             for b in range(n_blocks)], axis=0)
        m = jnp.max(s, axis=1, keepdims=True)
        p = jnp.exp2((s - m) * (scale * _LOG2E))
        l = jnp.sum(p, axis=1, keepdims=True)
        p = p.astype(BF16)
        lse = m * scale + jnp.log(l)
        for b in range(n_blocks):
            o = jnp.dot(p[rows(b)], window_rows(v, b), preferred_element_type=F32) / l[rows(b)]
            r, n = divmod(b, nblk)
            dst = pl.ds(r + n * band * d, band, stride=d) if d > 1 else pl.ds(b * band, band)
            o_scr[g, dst, :] = o
            l_scr[g, dst, :] = jnp.broadcast_to(lse[rows(b)], (band, LANE))

    lses = [l_scr[g] for g in range(len(DIL_CONFIGS))]
    m = functools.reduce(jnp.maximum, lses)
    es = [jnp.exp(l - m) for l in lses]
    den = functools.reduce(lambda a, b: a + b, es)
    num = functools.reduce(lambda a, b: a + b, [(e / den) * o_scr[g] for g, e in enumerate(es)])
    o_ref[...] = (num * z_ref[...].astype(F32)).astype(BF16)


def _dilated(p_slots, offs):
    B, _, S, _ = p_slots.shape
    G = len(DIL_CONFIGS)
    hg = DIL_HEADS_PER_GROUP
    for window, d in DIL_CONFIGS:
        band = window // d
        assert band == LANE and S % (d * band) == 0

    def slab(base):
        return pl.BlockSpec((None, None, S, LANE), lambda b, h: (b, base + h, 0, 0))

    in_specs = ([slab(offs[f"{name}{g}"]) for name in ("qb", "kb", "vb") for g in range(G)]
                + [slab(offs["zb"])])
    return pl.pallas_call(
        _dilated_kernel,
        grid=(B, hg),
        in_specs=in_specs,
        out_specs=pl.BlockSpec((None, None, S, LANE), lambda b, h: (b, h, 0, 0)),
        out_shape=jax.ShapeDtypeStruct((B, hg, S, LANE), BF16),
        scratch_shapes=[pltpu.VMEM((G, S, LANE), F32)] * 2,
        compiler_params=pltpu.CompilerParams(
            dimension_semantics=("arbitrary", "arbitrary"),
            vmem_limit_bytes=VMEM_LIMIT_BYTES),
        name="dilated",
    )(*([p_slots] * (3 * G + 1)))


def _mem_kv_kernel(mem_ref, g_ref, w_ref, o_ref):
    h = _rms_norm(mem_ref[...], g_ref[...]).astype(BF16)
    o_ref[...] = jnp.dot(h, w_ref[...], preferred_element_type=F32).astype(BF16)


def _mem_kv(mem, g, w_bf):
    B, M, D = mem.shape
    width = w_bf.shape[1]
    return pl.pallas_call(
        _mem_kv_kernel,
        grid=(B,),
        in_specs=[
            pl.BlockSpec((None, M, D), lambda b: (b, 0, 0)),
            pl.BlockSpec((1, D), lambda b: (0, 0)),
            pl.BlockSpec((D, width), lambda b: (0, 0), pipeline_mode=pl.Buffered(1)),
        ],
        out_specs=pl.BlockSpec((None, M, width), lambda b: (b, 0, 0)),
        out_shape=jax.ShapeDtypeStruct((B, M, width), BF16),
        compiler_params=pltpu.CompilerParams(
            dimension_semantics=("arbitrary",),
            vmem_limit_bytes=VMEM_LIMIT_BYTES),
        name="mem_kv",
    )(mem, g.reshape(1, D), w_bf)


def _out_proj_kernel(x_ref, ya_ref, yb_ref, qm_ref, zm_ref, ga0, ga1, gb0, gb1, gm0, gm1, mkv_ref,
                     wa_ref, wb_ref, wm_ref, wo_ref, gf_ref, o_ref, *, final_norm):
    def slots(ref):
        return jnp.concatenate([ref[s] for s in range(ref.shape[0])], axis=1)

    w_m = MEM_HEADS * MEM_HEAD_DIM
    scale = MEM_HEAD_DIM ** -0.5
    qm = slots(qm_ref)
    zm = slots(zm_ref)
    ym = []
    for h in range(MEM_HEADS):
        c0, c1 = h * MEM_HEAD_DIM, (h + 1) * MEM_HEAD_DIM
        s = lax.dot_general(qm[:, c0:c1], mkv_ref[:, c0:c1], _CONTRACT_LAST,
                            preferred_element_type=F32) * scale
        m = jnp.max(s, axis=1, keepdims=True)
        p = jnp.exp(s - m)
        l = jnp.sum(p, axis=1, keepdims=True)
        pv = jnp.dot(p.astype(BF16), mkv_ref[:, w_m + c0:w_m + c1], preferred_element_type=F32)
        ym.append(((pv / l) * zm[:, c0:c1].astype(F32)).astype(BF16))
    ym = jnp.concatenate(ym, axis=1)

    def gate(r0, r1):
        return jnp.concatenate([slots(r0), slots(r1)], axis=1).astype(F32)

    merged = (gate(ga0, ga1) * jnp.dot(slots(ya_ref), wa_ref[...], preferred_element_type=F32)
              + gate(gb0, gb1) * jnp.dot(slots(yb_ref), wb_ref[...], preferred_element_type=F32)
              + gate(gm0, gm1) * jnp.dot(ym, wm_ref[...], preferred_element_type=F32))
    y = x_ref[...] + jnp.dot(merged.astype(BF16), wo_ref[...], preferred_element_type=F32)
    o_ref[...] = _rms_norm(y, gf_ref[...]) if final_norm else y


def _out_proj(x, ya, yb, p_slots, offs, mkv, wa, wb, wm, wo, g_final, final_norm):
    B, S, D = x.shape
    tm = OUT_TM
    gslots = D // LANE // 2
    assert S % tm == 0 and offs["gates"] % gslots == 0 and gslots == MOBA_HEADS
    assert offs["qm"] % gslots == 0 and offs["zm"] % gslots == 0

    def pblock(slot0):
        blk = slot0 // gslots
        return pl.BlockSpec((None, gslots, tm, LANE), lambda b, s: (b, blk, s, 0))

    def resident(w):
        return pl.BlockSpec(w.shape, lambda b, s: (0,) * w.ndim, pipeline_mode=pl.Buffered(1))

    gate_specs = [pblock(offs["gates"] + i * gslots) for i in range(2 * N_BRANCH)]
    return pl.pallas_call(
        functools.partial(_out_proj_kernel, final_norm=final_norm),
        grid=(B, S // tm),
        in_specs=[
            pl.BlockSpec((None, tm, D), lambda b, s: (b, s, 0)),
            pl.BlockSpec((None, MOBA_HEADS, tm, LANE), lambda b, s: (b, 0, s, 0)),
            pl.BlockSpec((None, DIL_HEADS_PER_GROUP, tm, LANE), lambda b, s: (b, 0, s, 0)),
            pblock(offs["qm"]), pblock(offs["zm"]), *gate_specs,
            pl.BlockSpec((None,) + mkv.shape[1:], lambda b, s: (b, 0, 0)),
            resident(wa), resident(wb), resident(wm), resident(wo),
            pl.BlockSpec((1, D), lambda b, s: (0, 0)),
        ],
        out_specs=pl.BlockSpec((None, tm, D), lambda b, s: (b, s, 0)),
        out_shape=jax.ShapeDtypeStruct((B, S, D), F32),
        compiler_params=pltpu.CompilerParams(
            dimension_semantics=("arbitrary", "arbitrary"),
            vmem_limit_bytes=VMEM_LIMIT_BYTES),
        name="out_proj",
    )(x, ya, yb, *([p_slots] * (2 + 2 * N_BRANCH)), mkv, wa, wb, wm, wo, g_final.reshape(1, D))


def _rope_tables(S):
    half = HEAD_DIM // 2
    inv = ROPE_THETA ** (-jnp.arange(half, dtype=F32) / half)
    ang = jnp.arange(S, dtype=F32)[:, None] * inv[None, :]
    cos, sin = jnp.cos(ang), jnp.sin(ang)
    return jnp.concatenate([cos, cos], axis=1), jnp.concatenate([-sin, sin], axis=1)


def kernel(x, mem, norm_in_g, norm_mem_g, w_in, w_mem_kv, w_proj_a, w_proj_b, w_proj_m, w_out, norm_final_g):
    B, S, D = x.shape
    depth = w_in.shape[0]
    offs, n_slots = _slot_offsets(D)
    assert w_in.shape[2] == n_slots * LANE
    cos_t, sin_t = _rope_tables(S)
    for layer in range(depth):
        p_slots = _in_proj(x, norm_in_g[layer], w_in[layer], cos_t, sin_t)
        ya = _moba(p_slots, offs)
        yb = _dilated(p_slots, offs)
        mkv = _mem_kv(mem, norm_mem_g, w_mem_kv[layer].astype(BF16))
        x = _out_proj(x, ya, yb, p_slots, offs, mkv,
                      w_proj_a[layer].astype(BF16), w_proj_b[layer].astype(BF16),
                      w_proj_m[layer].astype(BF16), w_out[layer].astype(BF16),
                      norm_final_g, final_norm=(layer == depth - 1))
    return x
```

```python
import functools

import jax
import jax.numpy as jnp
from jax import lax
from jax.experimental import pallas as pl
from jax.experimental.pallas import tpu as pltpu

F32 = jnp.float32
BF16 = jnp.bfloat16

HEAD_DIM = 128
MOBA_HEADS = 8
MOBA_BLOCK = 256
MOBA_TOPK = 3
DIL_CONFIGS = ((128, 1), (512, 4), (2048, 16))
DIL_HEADS_PER_GROUP = 4
MEM_HEADS = 4
MEM_HEAD_DIM = 256
ROPE_THETA = 10000.0
RMS_EPS = 1e-6
NEG = -1e30
N_BRANCH = 3

LANE = 128
F32_SUBLANES = 8
BF16_SUBLANES = 16
V7X_VMEM_BYTES = 64 * 1024 * 1024
VMEM_LIMIT_BYTES = V7X_VMEM_BYTES - 8 * 1024 * 1024

IN_TN = 512
IN_CM = 256
IN_STAGE_BUFS = 8
OUT_TM = 256

_CONTRACT_LAST = (((1,), (1,)), ((), ()))
_LOG2E = 1.4426950408889634


def _segments(d_model):
    w_a = MOBA_HEADS * HEAD_DIM
    w_g = DIL_HEADS_PER_GROUP * HEAD_DIM
    w_m = MEM_HEADS * MEM_HEAD_DIM

    def groups(name, kind):
        return tuple((f"{name}{g}", w_g, kind, d) for g, (_, d) in enumerate(DIL_CONFIGS))

    return (("qa", w_a, "rope", 1), ("ka", w_a, "rope", 1), ("va", w_a, "ident", 1), ("za", w_a, "silu", 1),
            *groups("qb", "rope"), *groups("kb", "rope"), *groups("vb", "ident"), ("zb", w_g, "silu", 1),
            ("qm", w_m, "ident", 1), ("zm", w_m, "silu", 1), ("gates", N_BRANCH * d_model, "sigmoid", 1))


def _slot_offsets(d_model):
    offs, col = {}, 0
    for name, width, _, _ in _segments(d_model):
        offs[name] = col // LANE
        col += width
    return offs, col // LANE


def _rms_norm(xf, g):
    ms = jnp.mean(xf * xf, axis=-1, keepdims=True)
    return (xf * lax.rsqrt(ms + RMS_EPS)) * g


def _in_proj_kernel(x_hbm, g_ref, w_ref, cos_ref, sin_ref, o_ref, x_buf, h_ref, stage_ref, x_sem, *,
                    tile_kinds):
    i = pl.program_id(0)
    j = pl.program_id(1)
    n_slots = o_ref.shape[0]
    S = h_ref.shape[0]

    def x_copy(b):
        return pltpu.make_async_copy(x_hbm.at[b], x_buf, x_sem)

    @pl.when((i == 0) & (j == 0))
    def _():
        x_copy(0).start()

    @pl.when(j == 0)
    def _():
        x_copy(i).wait()
        h_ref[...] = _rms_norm(x_buf[...], g_ref[...]).astype(BF16)

    @pl.when((j == 1) & (i + 1 < pl.num_programs(0)))
    def _():
        x_copy(i + 1).start()

    def in_tiles(ranges):
        pred = None
        for lo, hi in ranges:
            p = (j >= lo) & (j < hi)
            pred = p if pred is None else (pred | p)
        return pred

    def run(epilogue, d):
        w = w_ref[...].astype(BF16)
        n = IN_CM // d
        for c in range(S // IN_CM):
            rows = pl.ds(c * IN_CM, IN_CM)
            acc = jnp.dot(h_ref[rows, :], w, preferred_element_type=F32)
            for s in range(n_slots):
                e = epilogue(acc[:, s * LANE:(s + 1) * LANE], rows)
                if d == 1:
                    o_ref[s, rows, :] = e.astype(BF16)
                    continue
                pair = 2 * ((c * n_slots + s) % (stage_ref.shape[0] // 2))
                stage, stage2 = stage_ref.at[pair], stage_ref.at[pair + 1]
                stage[...] = e
                if d == 16:
                    quarter = IN_CM // 4
                    for r4 in range(4):
                        stage2[pl.ds(r4 * quarter, quarter), :] = stage[pl.ds(r4, quarter, stride=4), :]
                    parts = {r4 + 4 * r2: stage2[pl.ds(r4 * quarter + r2, n, stride=4), :]
                             for r4 in range(4) for r2 in range(4)}
                else:
                    parts = {r: stage[pl.ds(r, n, stride=d), :] for r in range(d)}
                for r, part in parts.items():
                    o_ref[s, pl.ds(r * (S // d) + c * n, n), :] = part.astype(BF16)

    epilogues = {
        "rope": lambda a, rows: a * cos_ref[rows, :] + pltpu.roll(a, HEAD_DIM // 2, 1) * sin_ref[rows, :],
        "ident": lambda a, rows: a,
        "silu": lambda a, rows: a * jax.nn.sigmoid(a),
        "sigmoid": lambda a, rows: jax.nn.sigmoid(a),
    }
    for (kind, d), ranges in tile_kinds.items():
        pl.when(in_tiles(ranges))(functools.partial(run, epilogues[kind], d))


def _in_proj(x, g, w, cos_t, sin_t):
    B, S, D = x.shape
    width = w.shape[1]
    tn = IN_TN
    assert S % IN_CM == 0 and width % tn == 0 and tn % LANE == 0
    tile_kinds = {}
    col = 0
    for _, seg_w, kind, d in _segments(D):
        assert col % tn == 0 and seg_w % tn == 0 and (IN_CM // d) % BF16_SUBLANES == 0 and d in (1, 4, 16)
        tile_kinds.setdefault((kind, d), []).append((col // tn, (col + seg_w) // tn))
        col += seg_w
    assert col == width

    def table_spec():
        return pl.BlockSpec((S, LANE), lambda i, j: (0, 0), pipeline_mode=pl.Buffered(1))

    return pl.pallas_call(
        functools.partial(_in_proj_kernel, tile_kinds=tile_kinds),
        grid=(B, width // tn),
        in_specs=[
            pl.BlockSpec(memory_space=pltpu.HBM),
            pl.BlockSpec((1, D), lambda i, j: (0, 0)),
            pl.BlockSpec((D, tn), lambda i, j: (0, j)),
            table_spec(), table_spec(),
        ],
        out_specs=pl.BlockSpec((None, tn // LANE, S, LANE), lambda i, j: (i, j, 0, 0)),
        out_shape=jax.ShapeDtypeStruct((B, width // LANE, S, LANE), BF16),
        scratch_shapes=[pltpu.VMEM((S, D), F32), pltpu.VMEM((S, D), BF16),
                        pltpu.VMEM((IN_STAGE_BUFS, IN_CM, LANE), F32), pltpu.SemaphoreType.DMA(())],
        compiler_params=pltpu.CompilerParams(
            dimension_semantics=("arbitrary", "arbitrary"),
            vmem_limit_bytes=VMEM_LIMIT_BYTES),
        name="in_proj",
    )(x, g.reshape(1, D), w, cos_t, sin_t)


def _reduce_rows(op, a):
    rows, n = a.shape
    chain = 8
    a = a.reshape(rows // (chain * F32_SUBLANES), chain, F32_SUBLANES, n)
    return op(op(op(a, axis=1), axis=0), axis=0, keepdims=True)


def _moba_kernel(q_ref, k_ref, v_ref, z_ref, o_ref):
    S = q_ref.shape[0]
    blk = MOBA_BLOCK
    nb = S // blk
    scale = HEAD_DIM ** -0.5
    k = k_ref[...]
    q_t = q_ref[...].astype(F32).T.astype(BF16)
    v_t = v_ref[...].astype(F32).T.astype(BF16)

    kmean = jnp.mean(k.astype(F32).reshape(nb, blk, HEAD_DIM), axis=1)
    k_hi = kmean.astype(BF16).astype(F32)
    k_hl = jnp.concatenate([k_hi, kmean - k_hi], axis=0).astype(BF16)
    gate2 = jnp.dot(k_hl, q_t, preferred_element_type=F32)
    gate = gate2[:nb] + gate2[nb:]

    key_row = lax.broadcasted_iota(jnp.int32, (blk, blk), 0)
    query_col = lax.broadcasted_iota(jnp.int32, (blk, blk), 1)
    blk_row = lax.broadcasted_iota(jnp.int32, (nb, blk), 0)

    key_blk = lax.broadcasted_iota(jnp.int32, (S, LANE), 0) // blk
    onehot = (key_blk == lax.broadcasted_iota(jnp.int32, (S, LANE), 1)).astype(BF16)
    k_aug = jnp.concatenate([k, onehot], axis=1)

    def cols(i):
        return slice(i * blk, (i + 1) * blk)

    def scores(i):
        kv_len = (i + 1) * blk
        if i == 0:
            s = jnp.dot(k[:kv_len], q_t[:, cols(i)], preferred_element_type=F32)
            return jnp.where(key_row <= query_col, s, NEG)
        g = gate[:, cols(i)]
        rank = jnp.zeros((nb, blk), jnp.int32)
        for jp in range(i):
            c = g[jp:jp + 1, :]
            beats = (c > g) | ((c == g) & (blk_row > jp))
            rank = rank + beats.astype(jnp.int32)
        bias = jnp.where((rank >= MOBA_TOPK) & (blk_row < i), NEG, 0.0)
        bias = jnp.concatenate([bias, jnp.zeros((LANE - nb, blk), F32)], axis=0).astype(BF16)
        q_aug = jnp.concatenate([q_t[:, cols(i)], bias], axis=0)
        s = jnp.dot(k_aug[:kv_len], q_aug, preferred_element_type=F32)
        own = jnp.where(key_row <= query_col, s[i * blk:], NEG)
        return jnp.concatenate([s[:i * blk], own], axis=0)

    s_all = [scores(i) for i in range(nb)]
    m = [_reduce_rows(jnp.max, s) for s in s_all]
    p = [jnp.exp2((s - mi) * (scale * _LOG2E)) for s, mi in zip(s_all, m)]
    l = [_reduce_rows(jnp.sum, pi) for pi in p]
    pv_t = [jnp.dot(v_t[:, :(i + 1) * blk], p[i].astype(BF16), preferred_element_type=F32)
            for i in range(nb)]
    for i in range(nb):
        out = (pv_t[i] / l[i]).T * z_ref[cols(i), :].astype(F32)
        o_ref[cols(i), :] = out.astype(BF16)


def _moba(p_slots, offs):
    B, _, S, _ = p_slots.shape
    assert S % MOBA_BLOCK == 0 and S // MOBA_BLOCK <= LANE

    def slab(base):
        return pl.BlockSpec((None, None, S, LANE), lambda b, h: (b, base + h, 0, 0))

    return pl.pallas_call(
        _moba_kernel,
        grid=(B, MOBA_HEADS),
        in_specs=[slab(offs["qa"]), slab(offs["ka"]), slab(offs["va"]), slab(offs["za"])],
        out_specs=pl.BlockSpec((None, None, S, LANE), lambda b, h: (b, h, 0, 0)),
        out_shape=jax.ShapeDtypeStruct((B, MOBA_HEADS, S, LANE), BF16),
        compiler_params=pltpu.CompilerParams(
            dimension_semantics=("arbitrary", "arbitrary"),
            vmem_limit_bytes=VMEM_LIMIT_BYTES),
        name="moba",
    )(p_slots, p_slots, p_slots, p_slots)


def _dilated_kernel(q0, q1, q2, k0, k1, k2, v0, v1, v2, z_ref, o_ref, o_scr, l_scr):
    S = z_ref.shape[0]
    scale = HEAD_DIM ** -0.5
    q_refs, k_refs, v_refs = (q0, q1, q2), (k0, k1, k2), (v0, v1, v2)

    band = LANE
    n_blocks = S // band
    row = lax.broadcasted_iota(jnp.int32, (band, 2 * band), 0)
    col = lax.broadcasted_iota(jnp.int32, (band, 2 * band), 1)
    mask_rest = (col >= row) & (col <= row + band)
    mask_first = mask_rest & (col >= band)

    for g, (window, d) in enumerate(DIL_CONFIGS):
        assert window // d == band
        nblk = S // d // band
        q = q_refs[g][...]
        k = k_refs[g][...]
        v = v_refs[g][...]

        def rows(b):
            return slice(b * band, (b + 1) * band)

        def window_rows(t, b):
            if b % nblk == 0:
                return jnp.concatenate([t[rows(b)], t[rows(b)]], axis=0)
            return t[(b - 1) * band:(b + 1) * band]

        s = jnp.concatenate(
            [jnp.where(mask_first if b % nblk == 0 else mask_rest,
                       lax.dot_general(q[rows(b)], window_rows(k, b), _CONTRACT_LAST,
                                       preferred_element_type=F32), NEG)
             for b in range(n_blocks)], axis=0)
        m = jnp.max(s, axis=1, keepdims=True)
        p = jnp.exp2((s - m) * (scale * _LOG2E))
        l = jnp.sum(p, axis=1, keepdims=True)
        p = p.astype(BF16)
        lse = m * scale + jnp.log(l)
        for b in range(n_blocks):
            o = jnp.dot(p[rows(b)], window_rows(v, b), preferred_element_type=F32) / l[rows(b)]
            r, n = divmod(b, nblk)
            dst = pl.ds(r + n * band * d, band, stride=d) if d > 1 else pl.ds(b * band, band)
            o_scr[g, dst, :] = o
            l_scr[g, dst, :] = jnp.broadcast_to(lse[rows(b)], (band, LANE))

    lses = [l_scr[g] for g in range(len(DIL_CONFIGS))]
    m = functools.reduce(jnp.maximum, lses)
    es = [jnp.exp(l - m) for l in lses]
    den = functools.reduce(lambda a, b: a + b, es)
    num = functools.reduce(lambda a, b: a + b, [(e / den) * o_scr[g] for g, e in enumerate(es)])
    o_ref[...] = (num * z_ref[...].astype(F32)).astype(BF16)


def _dilated(p_slots, offs):
    B, _, S, _ = p_slots.shape
    G = len(DIL_CONFIGS)
    hg = DIL_HEADS_PER_GROUP
    for window, d in DIL_CONFIGS:
        band = window // d
        assert band == LANE and S % (d * band) == 0

    def slab(base):
        return pl.BlockSpec((None, None, S, LANE), lambda b, h: (b, base + h, 0, 0))

    in_specs = ([slab(offs[f"{name}{g}"]) for name in ("qb", "kb", "vb") for g in range(G)]
                + [slab(offs["zb"])])
    return pl.pallas_call(
        _dilated_kernel,
        grid=(B, hg),
        in_specs=in_specs,
        out_specs=pl.BlockSpec((None, None, S, LANE), lambda b, h: (b, h, 0, 0)),
        out_shape=jax.ShapeDtypeStruct((B, hg, S, LANE), BF16),
        scratch_shapes=[pltpu.VMEM((G, S, LANE), F32)] * 2,
        compiler_params=pltpu.CompilerParams(
            dimension_semantics=("arbitrary", "arbitrary"),
            vmem_limit_bytes=VMEM_LIMIT_BYTES),
        name="dilated",
    )(*([p_slots] * (3 * G + 1)))


def _mem_kv_kernel(mem_ref, g_ref, w_ref, o_ref):
    h = _rms_norm(mem_ref[...], g_ref[...]).astype(BF16)
    o_ref[...] = jnp.dot(h, w_ref[...], preferred_element_type=F32).astype(BF16)


def _mem_kv(mem, g, w_bf):
    B, M, D = mem.shape
    width = w_bf.shape[1]
    return pl.pallas_call(
        _mem_kv_kernel,
        grid=(B,),
        in_specs=[
            pl.BlockSpec((None, M, D), lambda b: (b, 0, 0)),
            pl.BlockSpec((1, D), lambda b: (0, 0)),
            pl.BlockSpec((D, width), lambda b: (0, 0), pipeline_mode=pl.Buffered(1)),
        ],
        out_specs=pl.BlockSpec((None, M, width), lambda b: (b, 0, 0)),
        out_shape=jax.ShapeDtypeStruct((B, M, width), BF16),
        compiler_params=pltpu.CompilerParams(
            dimension_semantics=("arbitrary",),
            vmem_limit_bytes=VMEM_LIMIT_BYTES),
        name="mem_kv",
    )(mem, g.reshape(1, D), w_bf)


def _out_proj_kernel(x_ref, ya_ref, yb_ref, qm_ref, zm_ref, ga0, ga1, gb0, gb1, gm0, gm1, mkv_ref,
                     wa_ref, wb_ref, wm_ref, wo_ref, gf_ref, o_ref, *, final_norm):
    def slots(ref):
        return jnp.concatenate([ref[s] for s in range(ref.shape[0])], axis=1)

    w_m = MEM_HEADS * MEM_HEAD_DIM
    scale = MEM_HEAD_DIM ** -0.5
    qm = slots(qm_ref)
    zm = slots(zm_ref)
    ym = []
    for h in range(MEM_HEADS):
        c0, c1 = h * MEM_HEAD_DIM, (h + 1) * MEM_HEAD_DIM
        s = lax.dot_general(qm[:, c0:c1], mkv_ref[:, c0:c1], _CONTRACT_LAST,
                            preferred_element_type=F32) * scale
        m = jnp.max(s, axis=1, keepdims=True)
        p = jnp.exp(s - m)
        l = jnp.sum(p, axis=1, keepdims=True)
        pv = jnp.dot(p.astype(BF16), mkv_ref[:, w_m + c0:w_m + c1], preferred_element_type=F32)
        ym.append(((pv / l) * zm[:, c0:c1].astype(F32)).astype(BF16))
    ym = jnp.concatenate(ym, axis=1)

    def gate(r0, r1):
        return jnp.concatenate([slots(r0), slots(r1)], axis=1).astype(F32)

    merged = (gate(ga0, ga1) * jnp.dot(slots(ya_ref), wa_ref[...], preferred_element_type=F32)
              + gate(gb0, gb1) * jnp.dot(slots(yb_ref), wb_ref[...], preferred_element_type=F32)
              + gate(gm0, gm1) * jnp.dot(ym, wm_ref[...], preferred_element_type=F32))
    y = x_ref[...] + jnp.dot(merged.astype(BF16), wo_ref[...], preferred_element_type=F32)
    o_ref[...] = _rms_norm(y, gf_ref[...]) if final_norm else y


def _out_proj(x, ya, yb, p_slots, offs, mkv, wa, wb, wm, wo, g_final, final_norm):
    B, S, D = x.shape
    tm = OUT_TM
    gslots = D // LANE // 2
    assert S % tm == 0 and offs["gates"] % gslots == 0 and gslots == MOBA_HEADS
    assert offs["qm"] % gslots == 0 and offs["zm"] % gslots == 0

    def pblock(slot0):
        blk = slot0 // gslots
        return pl.BlockSpec((None, gslots, tm, LANE), lambda b, s: (b, blk, s, 0))

    def resident(w):
        return pl.BlockSpec(w.shape, lambda b, s: (0,) * w.ndim, pipeline_mode=pl.Buffered(1))

    gate_specs = [pblock(offs["gates"] + i * gslots) for i in range(2 * N_BRANCH)]
    return pl.pallas_call(
        functools.partial(_out_proj_kernel, final_norm=final_norm),
        grid=(B, S // tm),
        in_specs=[
            pl.BlockSpec((None, tm, D), lambda b, s: (b, s, 0)),
            pl.BlockSpec((None, MOBA_HEADS, tm, LANE), lambda b, s: (b, 0, s, 0)),
            pl.BlockSpec((None, DIL_HEADS_PER_GROUP, tm, LANE), lambda b, s: (b, 0, s, 0)),
            pblock(offs["qm"]), pblock(offs["zm"]), *gate_specs,
            pl.BlockSpec((None,) + mkv.shape[1:], lambda b, s: (b, 0, 0)),
            resident(wa), resident(wb), resident(wm), resident(wo),
            pl.BlockSpec((1, D), lambda b, s: (0, 0)),
        ],
        out_specs=pl.BlockSpec((None, tm, D), lambda b, s: (b, s, 0)),
        out_shape=jax.ShapeDtypeStruct((B, S, D), F32),
        compiler_params=pltpu.CompilerParams(
            dimension_semantics=("arbitrary", "arbitrary"),
            vmem_limit_bytes=VMEM_LIMIT_BYTES),
        name="out_proj",
    )(x, ya, yb, *([p_slots] * (2 + 2 * N_BRANCH)), mkv, wa, wb, wm, wo, g_final.reshape(1, D))


def _rope_tables(S):
    half = HEAD_DIM // 2
    inv = ROPE_THETA ** (-jnp.arange(half, dtype=F32) / half)
    ang = jnp.arange(S, dtype=F32)[:, None] * inv[None, :]
    cos, sin = jnp.cos(ang), jnp.sin(ang)
    return jnp.concatenate([cos, cos], axis=1), jnp.concatenate([-sin, sin], axis=1)


def kernel(x, mem, norm_in_g, norm_mem_g, w_in, w_mem_kv, w_proj_a, w_proj_b, w_proj_m, w_out, norm_final_g):
    B, S, D = x.shape
    depth = w_in.shape[0]
    offs, n_slots = _slot_offsets(D)
    assert w_in.shape[2] == n_slots * LANE
    cos_t, sin_t = _rope_tables(S)
    for layer in range(depth):
        p_slots = _in_proj(x, norm_in_g[layer], w_in[layer], cos_t, sin_t)
        ya = _moba(p_slots, offs)
        yb = _dilated(p_slots, offs)
        mkv = _mem_kv(mem, norm_mem_g, w_mem_kv[layer].astype(BF16))
        x = _out_proj(x, ya, yb, p_slots, offs, mkv,
                      w_proj_a[layer].astype(BF16), w_proj_b[layer].astype(BF16),
                      w_proj_m[layer].astype(BF16), w_out[layer].astype(BF16),
                      norm_final_g, final_norm=(layer == depth - 1))
    return x
```

```python
import functools

import jax
import jax.numpy as jnp
from jax import lax
from jax.experimental import pallas as pl
from jax.experimental.pallas import tpu as pltpu

F32 = jnp.float32
BF16 = jnp.bfloat16

HEAD_DIM = 128
MOBA_HEADS = 8
MOBA_BLOCK = 256
MOBA_TOPK = 3
DIL_CONFIGS = ((128, 1), (512, 4), (2048, 16))
DIL_HEADS_PER_GROUP = 4
MEM_HEADS = 4
MEM_HEAD_DIM = 256
ROPE_THETA = 10000.0
RMS_EPS = 1e-6
NEG = -1e30
N_BRANCH = 3

LANE = 128
F32_SUBLANES = 8
BF16_SUBLANES = 16
V7X_VMEM_BYTES = 64 * 1024 * 1024
VMEM_LIMIT_BYTES = V7X_VMEM_BYTES - 8 * 1024 * 1024

IN_TN = 512
IN_CM = 256
IN_STAGE_BUFS = 8
OUT_TM = 256

_CONTRACT_LAST = (((1,), (1,)), ((), ()))
_LOG2E = 1.4426950408889634


def _segments(d_model):
    w_a = MOBA_HEADS * HEAD_DIM
    w_g = DIL_HEADS_PER_GROUP * HEAD_DIM
    w_m = MEM_HEADS * MEM_HEAD_DIM

    def groups(name, kind):
        return tuple((f"{name}{g}", w_g, kind, d) for g, (_, d) in enumerate(DIL_CONFIGS))

    return (("qa", w_a, "rope", 1), ("ka", w_a, "rope", 1), ("va", w_a, "ident", 1), ("za", w_a, "silu", 1),
            *groups("qb", "rope"), *groups("kb", "rope"), *groups("vb", "ident"), ("zb", w_g, "silu", 1),
            ("qm", w_m, "ident", 1), ("zm", w_m, "silu", 1), ("gates", N_BRANCH * d_model, "sigmoid", 1))


def _slot_offsets(d_model):
    offs, col = {}, 0
    for name, width, _, _ in _segments(d_model):
        offs[name] = col // LANE
        col += width
    return offs, col // LANE


def _rms_norm(xf, g):
    ms = jnp.mean(xf * xf, axis=-1, keepdims=True)
    return (xf * lax.rsqrt(ms + RMS_EPS)) * g


def _in_proj_kernel(x_hbm, g_ref, w_ref, cos_ref, sin_ref, o_ref, x_buf, h_ref, stage_ref, x_sem, *,
                    tile_kinds):
    i = pl.program_id(0)
    j = pl.program_id(1)
    n_slots = o_ref.shape[0]
    S = h_ref.shape[0]

    def x_copy(b):
        return pltpu.make_async_copy(x_hbm.at[b], x_buf, x_sem)

    @pl.when((i == 0) & (j == 0))
    def _():
        x_copy(0).start()

    @pl.when(j == 0)
    def _():
        x_copy(i).wait()
        h_ref[...] = _rms_norm(x_buf[...], g_ref[...]).astype(BF16)

    @pl.when((j == 1) & (i + 1 < pl.num_programs(0)))
    def _():
        x_copy(i + 1).start()

    def in_tiles(ranges):
        pred = None
        for lo, hi in ranges:
            p = (j >= lo) & (j < hi)
            pred = p if pred is None else (pred | p)
        return pred

    def run(epilogue, d):
        w = w_ref[...].astype(BF16)
        n = IN_CM // d
        for c in range(S // IN_CM):
            rows = pl.ds(c * IN_CM, IN_CM)
            acc = jnp.dot(h_ref[rows, :], w, preferred_element_type=F32)
            for s in range(n_slots):
                e = epilogue(acc[:, s * LANE:(s + 1) * LANE], rows)
                if d == 1:
                    o_ref[s, rows, :] = e.astype(BF16)
                    continue
                pair = 2 * ((c * n_slots + s) % (stage_ref.shape[0] // 2))
                stage, stage2 = stage_ref.at[pair], stage_ref.at[pair + 1]
                stage[...] = e
                if d == 16:
                    quarter = IN_CM // 4
                    for r4 in range(4):
                        stage2[pl.ds(r4 * quarter, quarter), :] = stage[pl.ds(r4, quarter, stride=4), :]
                    parts = {r4 + 4 * r2: stage2[pl.ds(r4 * quarter + r2, n, stride=4), :]
                             for r4 in range(4) for r2 in range(4)}
                else:
                    parts = {r: stage[pl.ds(r, n, stride=d), :] for r in range(d)}
                for r, part in parts.items():
                    o_ref[s, pl.ds(r * (S // d) + c * n, n), :] = part.astype(BF16)

    epilogues = {
        "rope": lambda a, rows: a * cos_ref[rows, :] + pltpu.roll(a, HEAD_DIM // 2, 1) * sin_ref[rows, :],
        "ident": lambda a, rows: a,
        "silu": lambda a, rows: a * jax.nn.sigmoid(a),
        "sigmoid": lambda a, rows: jax.nn.sigmoid(a),
    }
    for (kind, d), ranges in tile_kinds.items():
        pl.when(in_tiles(ranges))(functools.partial(run, epilogues[kind], d))


def _in_proj(x, g, w, cos_t, sin_t):
    B, S, D = x.shape
    width = w.shape[1]
    tn = IN_TN
    assert S % IN_CM == 0 and width % tn == 0 and tn % LANE == 0
    tile_kinds = {}
    col = 0
    for _, seg_w, kind, d in _segments(D):
        assert col % tn == 0 and seg_w % tn == 0 and (IN_CM // d) % BF16_SUBLANES == 0 and d in (1, 4, 16)
        tile_kinds.setdefault((kind, d), []).append((col // tn, (col + seg_w) // tn))
        col += seg_w
    assert col == width

    def table_spec():
        return pl.BlockSpec((S, LANE), lambda i, j: (0, 0), pipeline_mode=pl.Buffered(1))

    return pl.pallas_call(
        functools.partial(_in_proj_kernel, tile_kinds=tile_kinds),
        grid=(B, width // tn),
        in_specs=[
            pl.BlockSpec(memory_space=pltpu.HBM),
            pl.BlockSpec((1, D), lambda i, j: (0, 0)),
            pl.BlockSpec((D, tn), lambda i, j: (0, j)),
            table_spec(), table_spec(),
        ],
        out_specs=pl.BlockSpec((None, tn // LANE, S, LANE), lambda i, j: (i, j, 0, 0)),
        out_shape=jax.ShapeDtypeStruct((B, width // LANE, S, LANE), BF16),
        scratch_shapes=[pltpu.VMEM((S, D), F32), pltpu.VMEM((S, D), BF16),
                        pltpu.VMEM((IN_STAGE_BUFS, IN_CM, LANE), F32), pltpu.SemaphoreType.DMA(())],
        compiler_params=pltpu.CompilerParams(
            dimension_semantics=("arbitrary", "arbitrary"),
            vmem_limit_bytes=VMEM_LIMIT_BYTES),
        name="in_proj",
    )(x, g.reshape(1, D), w, cos_t, sin_t)


def _reduce_rows(op, a):
    rows, n = a.shape
    chain = 8
    a = a.reshape(rows // (chain * F32_SUBLANES), chain, F32_SUBLANES, n)
    return op(op(op(a, axis=1), axis=0), axis=0, keepdims=True)


def _moba_kernel(q_ref, k_ref, v_ref, z_ref, o_ref):
    S = q_ref.shape[0]
    blk = MOBA_BLOCK
    nb = S // blk
    scale = HEAD_DIM ** -0.5
    k = k_ref[...]
    q_t = q_ref[...].astype(F32).T.astype(BF16)
    v_t = v_ref[...].astype(F32).T.astype(BF16)

    kmean = jnp.mean(k.astype(F32).reshape(nb, blk, HEAD_DIM), axis=1)
    k_hi = kmean.astype(BF16).astype(F32)
    k_hl = jnp.concatenate([k_hi, kmean - k_hi], axis=0).astype(BF16)
    gate2 = jnp.dot(k_hl, q_t, preferred_element_type=F32)
    gate = gate2[:nb] + gate2[nb:]

    key_row = lax.broadcasted_iota(jnp.int32, (blk, blk), 0)
    query_col = lax.broadcasted_iota(jnp.int32, (blk, blk), 1)
    blk_row = lax.broadcasted_iota(jnp.int32, (nb, blk), 0)

    key_blk = lax.broadcasted_iota(jnp.int32, (S, LANE), 0) // blk
    onehot = (key_blk == lax.broadcasted_iota(jnp.int32, (S, LANE), 1)).astype(BF16)
    k_aug = jnp.concatenate([k, onehot], axis=1)

    def cols(i):
        return slice(i * blk, (i + 1) * blk)

    def scores(i):
        kv_len = (i + 1) * blk
        if i == 0:
            s = jnp.dot(k[:kv_len], q_t[:, cols(i)], preferred_element_type=F32)
            return jnp.where(key_row <= query_col, s, NEG)
        g = gate[:, cols(i)]
        rank = jnp.zeros((nb, blk), jnp.int32)
        for jp in range(i):
            c = g[jp:jp + 1, :]
            beats = (c > g) | ((c == g) & (blk_row > jp))
            rank = rank + beats.astype(jnp.int32)
        bias = jnp.where((rank >= MOBA_TOPK) & (blk_row < i), NEG, 0.0)
        bias = jnp.concatenate([bias, jnp.zeros((LANE - nb, blk), F32)], axis=0).astype(BF16)
        q_aug = jnp.concatenate([q_t[:, cols(i)], bias], axis=0)
        s = jnp.dot(k_aug[:kv_len], q_aug, preferred_element_type=F32)
        own = jnp.where(key_row <= query_col, s[i * blk:], NEG)
        return jnp.concatenate([s[:i * blk], own], axis=0)

    s_all = [scores(i) for i in range(nb)]
    m = [_reduce_rows(jnp.max, s) for s in s_all]
    p = [jnp.exp2((s - mi) * (scale * _LOG2E)) for s, mi in zip(s_all, m)]
    l = [_reduce_rows(jnp.sum, pi) for pi in p]
    pv_t = [jnp.dot(v_t[:, :(i + 1) * blk], p[i].astype(BF16), preferred_element_type=F32)
            for i in range(nb)]
    for i in range(nb):
        out = (pv_t[i] / l[i]).T * z_ref[cols(i), :].astype(F32)
        o_ref[cols(i), :] = out.astype(BF16)


def _moba(p_slots, offs):
    B, _, S, _ = p_slots.shape
    assert S % MOBA_BLOCK == 0 and S // MOBA_BLOCK <= LANE

    def slab(base):
        return pl.BlockSpec((None, None, S, LANE), lambda b, h: (b, base + h, 0, 0))

    return pl.pallas_call(
        _moba_kernel,
        grid=(B, MOBA_HEADS),
        in_specs=[slab(offs["qa"]), slab(offs["ka"]), slab(offs["va"]), slab(offs["za"])],
        out_specs=pl.BlockSpec((None, None, S, LANE), lambda b, h: (b, h, 0, 0)),
        out_shape=jax.ShapeDtypeStruct((B, MOBA_HEADS, S, LANE), BF16),
        compiler_params=pltpu.CompilerParams(
            dimension_semantics=("arbitrary", "arbitrary"),
            vmem_limit_bytes=VMEM_LIMIT_BYTES),
        name="moba",
    )(p_slots, p_slots, p_slots, p_slots)


def _dilated_kernel(q0, q1, q2, k0, k1, k2, v0, v1, v2, z_ref, o_ref, o_scr, l_scr):
    S = z_ref.shape[0]
    scale = HEAD_DIM ** -0.5
    q_refs, k_refs, v_refs = (q0, q1, q2), (k0, k1, k2), (v0, v1, v2)

    band = LANE
    n_blocks = S // band
    row = lax.broadcasted_iota(jnp.int32, (band, 2 * band), 0)
    col = lax.broadcasted_iota(jnp.int32, (band, 2 * band), 1)
    mask_rest = (col >= row) & (col <= row + band)
    mask_first = mask_rest & (col >= band)

    for g, (window, d) in enumerate(DIL_CONFIGS):
        assert window // d == band
        nblk = S // d // band
        q = q_refs[g][...]
        k = k_refs[g][...]
        v = v_refs[g][...]

        def rows(b):
            return slice(b * band, (b + 1) * band)

        def window_rows(t, b):
            if b % nblk == 0:
                return jnp.concatenate([t[rows(b)], t[rows(b)]], axis=0)
            return t[(b - 1) * band:(b + 1) * band]

        s = jnp.concatenate(
            [jnp.where(mask_first if b % nblk == 0 else mask_rest,
                       lax.dot_general(q[rows(b)], window_rows(k, b), _CONTRACT_LAST,
                                       preferred_element_type=F32), NEG)
             for b in range(n_blocks)], axis=0)
        m = jnp.max(s, axis=1, keepdims=True)
        p = jnp.exp2((s - m) * (scale * _LOG2E))
        l = jnp.sum(p, axis=1, keepdims=True)
        p = p.astype(BF16)
        lse = m * scale + jnp.log(l)
        for b in range(n_blocks):
            o = jnp.dot(p[rows(b)], window_rows(v, b), preferred_element_type=F32) / l[rows(b)]
            r, n = divmod(b, nblk)
            dst = pl.ds(r + n * band * d, band, stride=d) if d > 1 else pl.ds(b * band, band)
            o_scr[g, dst, :] = o
            l_scr[g, dst, :] = jnp.broadcast_to(lse[rows(b)], (band, LANE))

    lses = [l_scr[g] for g in range(len(DIL_CONFIGS))]
    m = functools.reduce(jnp.maximum, lses)
    es = [jnp.exp(l - m) for l in lses]
    den = functools.reduce(lambda a, b: a + b, es)
    num = functools.reduce(lambda a, b: a + b, [(e / den) * o_scr[g] for g, e in enumerate(es)])
    o_ref[...] = (num * z_ref[...].astype(F32)).astype(BF16)


def _dilated(p_slots, offs):
    B, _, S, _ = p_slots.shape
    G = len(DIL_CONFIGS)
    hg = DIL_HEADS_PER_GROUP
    for window, d in DIL_CONFIGS:
        band = window // d
        assert band == LANE and S % (d * band) == 0

    def slab(base):
        return pl.BlockSpec((None, None, S, LANE), lambda b, h: (b, base + h, 0, 0))

    in_specs = ([slab(offs[f"{name}{g}"]) for name in ("qb", "kb", "vb") for g in range(G)]
                + [slab(offs["zb"])])
    return pl.pallas_call(
        _dilated_kernel,
        grid=(B, hg),
        in_specs=in_specs,
        out_specs=pl.BlockSpec((None, None, S, LANE), lambda b, h: (b, h, 0, 0)),
        out_shape=jax.ShapeDtypeStruct((B, hg, S, LANE), BF16),
        scratch_shapes=[pltpu.VMEM((G, S, LANE), F32)] * 2,
        compiler_params=pltpu.CompilerParams(
            dimension_semantics=("arbitrary", "arbitrary"),
            vmem_limit_bytes=VMEM_LIMIT_BYTES),
        name="dilated",
    )(*([p_slots] * (3 * G + 1)))


def _mem_kv_kernel(mem_ref, g_ref, w_ref, o_ref):
    h = _rms_norm(mem_ref[...], g_ref[...]).astype(BF16)
    o_ref[...] = jnp.dot(h, w_ref[...], preferred_element_type=F32).astype(BF16)


def _mem_kv(mem, g, w_bf):
    B, M, D = mem.shape
    width = w_bf.shape[1]
    return pl.pallas_call(
        _mem_kv_kernel,
        grid=(B,),
        in_specs=[
            pl.BlockSpec((None, M, D), lambda b: (b, 0, 0)),
            pl.BlockSpec((1, D), lambda b: (0, 0)),
            pl.BlockSpec((D, width), lambda b: (0, 0), pipeline_mode=pl.Buffered(1)),
        ],
        out_specs=pl.BlockSpec((None, M, width), lambda b: (b, 0, 0)),
        out_shape=jax.ShapeDtypeStruct((B, M, width), BF16),
        compiler_params=pltpu.CompilerParams(
            dimension_semantics=("arbitrary",),
            vmem_limit_bytes=VMEM_LIMIT_BYTES),
        name="mem_kv",
    )(mem, g.reshape(1, D), w_bf)


def _out_proj_kernel(x_ref, ya_ref, yb_ref, qm_ref, zm_ref, ga0, ga1, gb0, gb1, gm0, gm1, mkv_ref,
                     wa_ref, wb_ref, wm_ref, wo_ref, gf_ref, o_ref, *, final_norm):
    def slots(ref):
        return jnp.concatenate([ref[s] for s in range(ref.shape[0])], axis=1)

    def gate(r0, r1):
        return jnp.concatenate([slots(r0), slots(r1)], axis=1).astype(F32)

    merged = (gate(ga0, ga1) * jnp.dot(slots(ya_ref), wa_ref[...], preferred_element_type=F32)
              + gate(gb0, gb1) * jnp.dot(slots(yb_ref), wb_ref[...], preferred_element_type=F32))

    w_m = MEM_HEADS * MEM_HEAD_DIM
    scale = MEM_HEAD_DIM ** -0.5
    qm = slots(qm_ref)
    zm = slots(zm_ref)
    heads = [slice(h * MEM_HEAD_DIM, (h + 1) * MEM_HEAD_DIM) for h in range(MEM_HEADS)]
    s = [lax.dot_general(qm[:, c], mkv_ref[:, c], _CONTRACT_LAST, preferred_element_type=F32) for c in heads]
    m = [jnp.max(si, axis=1, keepdims=True) for si in s]
    p = [jnp.exp2((si - mi) * (scale * _LOG2E)) for si, mi in zip(s, m)]
    l = [jnp.sum(pi, axis=1, keepdims=True) for pi in p]
    pv = [jnp.dot(pi.astype(BF16), mkv_ref[:, w_m + c.start:w_m + c.stop], preferred_element_type=F32)
          for pi, c in zip(p, heads)]
    ym = jnp.concatenate([((pvi / li) * zm[:, c].astype(F32)).astype(BF16)
                          for pvi, li, c in zip(pv, l, heads)], axis=1)
    merged = merged + gate(gm0, gm1) * jnp.dot(ym, wm_ref[...], preferred_element_type=F32)
    y = x_ref[...] + jnp.dot(merged.astype(BF16), wo_ref[...], preferred_element_type=F32)
    o_ref[...] = _rms_norm(y, gf_ref[...]) if final_norm else y


def _out_proj(x, ya, yb, p_slots, offs, mkv, wa, wb, wm, wo, g_final, final_norm):
    B, S, D = x.shape
    tm = OUT_TM
    gslots = D // LANE // 2
    assert S % tm == 0 and offs["gates"] % gslots == 0 and gslots == MOBA_HEADS
    assert offs["qm"] % gslots == 0 and offs["zm"] % gslots == 0

    def pblock(slot0):
        blk = slot0 // gslots
        return pl.BlockSpec((None, gslots, tm, LANE), lambda b, s: (b, blk, s, 0))

    def resident(w):
        return pl.BlockSpec(w.shape, lambda b, s: (0,) * w.ndim, pipeline_mode=pl.Buffered(1))

    gate_specs = [pblock(offs["gates"] + i * gslots) for i in range(2 * N_BRANCH)]
    return pl.pallas_call(
        functools.partial(_out_proj_kernel, final_norm=final_norm),
        grid=(B, S // tm),
        in_specs=[
            pl.BlockSpec((None, tm, D), lambda b, s: (b, s, 0)),
            pl.BlockSpec((None, MOBA_HEADS, tm, LANE), lambda b, s: (b, 0, s, 0)),
            pl.BlockSpec((None, DIL_HEADS_PER_GROUP, tm, LANE), lambda b, s: (b, 0, s, 0)),
            pblock(offs["qm"]), pblock(offs["zm"]), *gate_specs,
            pl.BlockSpec((None,) + mkv.shape[1:], lambda b, s: (b, 0, 0)),
            resident(wa), resident(wb), resident(wm), resident(wo),
            pl.BlockSpec((1, D), lambda b, s: (0, 0)),
        ],
        out_specs=pl.BlockSpec((None, tm, D), lambda b, s: (b, s, 0)),
        out_shape=jax.ShapeDtypeStruct((B, S, D), F32),
        compiler_params=pltpu.CompilerParams(
            dimension_semantics=("arbitrary", "arbitrary"),
            vmem_limit_bytes=VMEM_LIMIT_BYTES),
        name="out_proj",
    )(x, ya, yb, *([p_slots] * (2 + 2 * N_BRANCH)), mkv, wa, wb, wm, wo, g_final.reshape(1, D))


def _rope_tables(S):
    half = HEAD_DIM // 2
    inv = ROPE_THETA ** (-jnp.arange(half, dtype=F32) / half)
    ang = jnp.arange(S, dtype=F32)[:, None] * inv[None, :]
    cos, sin = jnp.cos(ang), jnp.sin(ang)
    return jnp.concatenate([cos, cos], axis=1), jnp.concatenate([-sin, sin], axis=1)


def kernel(x, mem, norm_in_g, norm_mem_g, w_in, w_mem_kv, w_proj_a, w_proj_b, w_proj_m, w_out, norm_final_g):
    B, S, D = x.shape
    depth = w_in.shape[0]
    offs, n_slots = _slot_offsets(D)
    assert w_in.shape[2] == n_slots * LANE
    cos_t, sin_t = _rope_tables(S)
    for layer in range(depth):
        p_slots = _in_proj(x, norm_in_g[layer], w_in[layer], cos_t, sin_t)
        ya = _moba(p_slots, offs)
        yb = _dilated(p_slots, offs)
        mkv = _mem_kv(mem, norm_mem_g, w_mem_kv[layer].astype(BF16))
        x = _out_proj(x, ya, yb, p_slots, offs, mkv,
                      w_proj_a[layer].astype(BF16), w_proj_b[layer].astype(BF16),
                      w_proj_m[layer].astype(BF16), w_out[layer].astype(BF16),
                      norm_final_g, final_norm=(layer == depth - 1))
    return x
```

```python
import functools

import jax
import jax.numpy as jnp
from jax import lax
from jax.experimental import pallas as pl
from jax.experimental.pallas import tpu as pltpu

F32 = jnp.float32
BF16 = jnp.bfloat16

HEAD_DIM = 128
MOBA_HEADS = 8
MOBA_BLOCK = 256
MOBA_TOPK = 3
DIL_CONFIGS = ((128, 1), (512, 4), (2048, 16))
DIL_HEADS_PER_GROUP = 4
MEM_HEADS = 4
MEM_HEAD_DIM = 256
ROPE_THETA = 10000.0
RMS_EPS = 1e-6
NEG = -1e30
N_BRANCH = 3

LANE = 128
F32_SUBLANES = 8
BF16_SUBLANES = 16
V7X_VMEM_BYTES = 64 * 1024 * 1024
VMEM_LIMIT_BYTES = V7X_VMEM_BYTES - 8 * 1024 * 1024

IN_TN = 512
IN_CM = 512
IN_STAGE_BUFS = 8
OUT_TM = 256

_CONTRACT_LAST = (((1,), (1,)), ((), ()))
_LOG2E = 1.4426950408889634


def _segments(d_model):
    w_a = MOBA_HEADS * HEAD_DIM
    w_g = DIL_HEADS_PER_GROUP * HEAD_DIM
    w_m = MEM_HEADS * MEM_HEAD_DIM

    def groups(name, kind):
        return tuple((f"{name}{g}", w_g, kind, d) for g, (_, d) in enumerate(DIL_CONFIGS))

    return (("qa", w_a, "rope", 1), ("ka", w_a, "rope", 1), ("va", w_a, "ident", 1), ("za", w_a, "silu", 1),
            *groups("qb", "rope"), *groups("kb", "rope"), *groups("vb", "ident"), ("zb", w_g, "silu", 1),
            ("qm", w_m, "ident", 1), ("zm", w_m, "silu", 1), ("gates", N_BRANCH * d_model, "sigmoid", 1))


def _slot_offsets(d_model):
    offs, col = {}, 0
    for name, width, _, _ in _segments(d_model):
        offs[name] = col // LANE
        col += width
    return offs, col // LANE


def _rms_norm(xf, g):
    ms = jnp.mean(xf * xf, axis=-1, keepdims=True)
    return (xf * lax.rsqrt(ms + RMS_EPS)) * g


def _in_proj_kernel(x_hbm, g_ref, w_ref, cos_ref, sin_ref, o_ref, x_buf, h_ref, stage_ref, x_sem, *,
                    tile_kinds):
    i = pl.program_id(0)
    j = pl.program_id(1)
    n_slots = o_ref.shape[0]
    S = h_ref.shape[0]

    def x_copy(b):
        return pltpu.make_async_copy(x_hbm.at[b], x_buf, x_sem)

    @pl.when((i == 0) & (j == 0))
    def _():
        x_copy(0).start()

    @pl.when(j == 0)
    def _():
        x_copy(i).wait()
        h_ref[...] = _rms_norm(x_buf[...], g_ref[...]).astype(BF16)

    @pl.when((j == 1) & (i + 1 < pl.num_programs(0)))
    def _():
        x_copy(i + 1).start()

    def in_tiles(ranges):
        pred = None
        for lo, hi in ranges:
            p = (j >= lo) & (j < hi)
            pred = p if pred is None else (pred | p)
        return pred

    def run(epilogue, d):
        w = w_ref[...].astype(BF16)
        n = IN_CM // d
        for c in range(S // IN_CM):
            rows = pl.ds(c * IN_CM, IN_CM)
            acc = jnp.dot(h_ref[rows, :], w, preferred_element_type=F32)
            for s in range(n_slots):
                e = epilogue(acc[:, s * LANE:(s + 1) * LANE], rows)
                if d == 1:
                    o_ref[s, rows, :] = e.astype(BF16)
                    continue
                pair = 2 * ((c * n_slots + s) % (stage_ref.shape[0] // 2))
                stage, stage2 = stage_ref.at[pair], stage_ref.at[pair + 1]
                stage[...] = e
                if d == 16:
                    quarter = IN_CM // 4
                    for r4 in range(4):
                        stage2[pl.ds(r4 * quarter, quarter), :] = stage[pl.ds(r4, quarter, stride=4), :]
                    parts = {r4 + 4 * r2: stage2[pl.ds(r4 * quarter + r2, n, stride=4), :]
                             for r4 in range(4) for r2 in range(4)}
                else:
                    parts = {r: stage[pl.ds(r, n, stride=d), :] for r in range(d)}
                for r, part in parts.items():
                    o_ref[s, pl.ds(r * (S // d) + c * n, n), :] = part.astype(BF16)

    epilogues = {
        "rope": lambda a, rows: a * cos_ref[rows, :] + pltpu.roll(a, HEAD_DIM // 2, 1) * sin_ref[rows, :],
        "ident": lambda a, rows: a,
        "silu": lambda a, rows: a * jax.nn.sigmoid(a),
        "sigmoid": lambda a, rows: jax.nn.sigmoid(a),
    }
    for (kind, d), ranges in tile_kinds.items():
        pl.when(in_tiles(ranges))(functools.partial(run, epilogues[kind], d))


def _in_proj(x, g, w, cos_t, sin_t):
    B, S, D = x.shape
    width = w.shape[1]
    tn = IN_TN
    assert S % IN_CM == 0 and width % tn == 0 and tn % LANE == 0
    tile_kinds = {}
    col = 0
    for _, seg_w, kind, d in _segments(D):
        assert col % tn == 0 and seg_w % tn == 0 and (IN_CM // d) % BF16_SUBLANES == 0 and d in (1, 4, 16)
        tile_kinds.setdefault((kind, d), []).append((col // tn, (col + seg_w) // tn))
        col += seg_w
    assert col == width

    def table_spec():
        return pl.BlockSpec((S, LANE), lambda i, j: (0, 0), pipeline_mode=pl.Buffered(1))

    return pl.pallas_call(
        functools.partial(_in_proj_kernel, tile_kinds=tile_kinds),
        grid=(B, width // tn),
        in_specs=[
            pl.BlockSpec(memory_space=pltpu.HBM),
            pl.BlockSpec((1, D), lambda i, j: (0, 0)),
            pl.BlockSpec((D, tn), lambda i, j: (0, j)),
            table_spec(), table_spec(),
        ],
        out_specs=pl.BlockSpec((None, tn // LANE, S, LANE), lambda i, j: (i, j, 0, 0)),
        out_shape=jax.ShapeDtypeStruct((B, width // LANE, S, LANE), BF16),
        scratch_shapes=[pltpu.VMEM((S, D), F32), pltpu.VMEM((S, D), BF16),
                        pltpu.VMEM((IN_STAGE_BUFS, IN_CM, LANE), F32), pltpu.SemaphoreType.DMA(())],
        compiler_params=pltpu.CompilerParams(
            dimension_semantics=("arbitrary", "arbitrary"),
            vmem_limit_bytes=VMEM_LIMIT_BYTES),
        name="in_proj",
    )(x, g.reshape(1, D), w, cos_t, sin_t)


def _reduce_rows(op, a):
    rows, n = a.shape
    chain = 16
    a = a.reshape(rows // (chain * F32_SUBLANES), chain, F32_SUBLANES, n)
    return op(op(op(a, axis=1), axis=0), axis=0, keepdims=True)


def _moba_kernel(q_ref, k_ref, v_ref, z_ref, o_ref):
    S = q_ref.shape[0]
    blk = MOBA_BLOCK
    nb = S // blk
    scale = HEAD_DIM ** -0.5
    k = k_ref[...]
    q_t = q_ref[...].T
    v_t = v_ref[...].T

    kmean = jnp.mean(k.astype(F32).reshape(nb, blk, HEAD_DIM), axis=1)
    k_hi = kmean.astype(BF16).astype(F32)
    k_hl = jnp.concatenate([k_hi, kmean - k_hi], axis=0).astype(BF16)
    gate2 = jnp.dot(k_hl, q_t, preferred_element_type=F32)
    gate = gate2[:nb] + gate2[nb:]

    key_row = lax.broadcasted_iota(jnp.int32, (blk, blk), 0)
    query_col = lax.broadcasted_iota(jnp.int32, (blk, blk), 1)
    blk_row = lax.broadcasted_iota(jnp.int32, (nb, blk), 0)

    key_blk = lax.broadcasted_iota(jnp.int32, (S, LANE), 0) // blk
    onehot = (key_blk == lax.broadcasted_iota(jnp.int32, (S, LANE), 1)).astype(BF16)
    k_aug = jnp.concatenate([k, onehot], axis=1)

    def cols(i):
        return slice(i * blk, (i + 1) * blk)

    def scores(i):
        kv_len = (i + 1) * blk
        if i == 0:
            s = jnp.dot(k[:kv_len], q_t[:, cols(i)], preferred_element_type=F32)
            return jnp.where(key_row <= query_col, s, NEG)
        g = gate[:, cols(i)]
        rank = jnp.zeros((nb, blk), jnp.int32)
        for jp in range(i):
            c = g[jp:jp + 1, :]
            beats = (c > g) | ((c == g) & (blk_row > jp))
            rank = rank + beats.astype(jnp.int32)
        bias = jnp.where((rank >= MOBA_TOPK) & (blk_row < i), NEG, 0.0)
        bias = jnp.concatenate([bias, jnp.zeros((LANE - nb, blk), F32)], axis=0).astype(BF16)
        q_aug = jnp.concatenate([q_t[:, cols(i)], bias], axis=0)
        s = jnp.dot(k_aug[:kv_len], q_aug, preferred_element_type=F32)
        own = jnp.where(key_row <= query_col, s[i * blk:], NEG)
        return jnp.concatenate([s[:i * blk], own], axis=0)

    s_all = [scores(i) for i in range(nb)]
    m = [_reduce_rows(jnp.max, s) for s in s_all]
    p = [jnp.exp2((s - mi) * (scale * _LOG2E)) for s, mi in zip(s_all, m)]
    l = [_reduce_rows(jnp.sum, pi) for pi in p]
    pv_t = [jnp.dot(v_t[:, :(i + 1) * blk], p[i].astype(BF16), preferred_element_type=F32)
            for i in range(nb)]
    for i in range(nb):
        out = (pv_t[i] / l[i]).T * z_ref[cols(i), :].astype(F32)
        o_ref[cols(i), :] = out.astype(BF16)


def _moba(p_slots, offs):
    B, _, S, _ = p_slots.shape
    assert S % MOBA_BLOCK == 0 and S // MOBA_BLOCK <= LANE

    def slab(base):
        return pl.BlockSpec((None, None, S, LANE), lambda b, h: (b, base + h, 0, 0))

    return pl.pallas_call(
        _moba_kernel,
        grid=(B, MOBA_HEADS),
        in_specs=[slab(offs["qa"]), slab(offs["ka"]), slab(offs["va"]), slab(offs["za"])],
        out_specs=pl.BlockSpec((None, None, S, LANE), lambda b, h: (b, h, 0, 0)),
        out_shape=jax.ShapeDtypeStruct((B, MOBA_HEADS, S, LANE), BF16),
        compiler_params=pltpu.CompilerParams(
            dimension_semantics=("arbitrary", "arbitrary"),
            vmem_limit_bytes=VMEM_LIMIT_BYTES),
        name="moba",
    )(p_slots, p_slots, p_slots, p_slots)


def _dilated_kernel(q0, q1, q2, k0, k1, k2, v0, v1, v2, z_ref, o_ref, o_scr, l_scr):
    S = z_ref.shape[0]
    scale = HEAD_DIM ** -0.5
    q_refs, k_refs, v_refs = (q0, q1, q2), (k0, k1, k2), (v0, v1, v2)

    band = LANE
    n_blocks = S // band
    row = lax.broadcasted_iota(jnp.int32, (band, 2 * band), 0)
    col = lax.broadcasted_iota(jnp.int32, (band, 2 * band), 1)
    mask_rest = (col >= row) & (col <= row + band)
    mask_first = mask_rest & (col >= band)

    for g, (window, d) in enumerate(DIL_CONFIGS):
        assert window // d == band
        nblk = S // d // band
        q = q_refs[g][...]
        k = k_refs[g][...]
        v = v_refs[g][...]

        def rows(b):
            return slice(b * band, (b + 1) * band)

        def window_rows(t, b):
            if b % nblk == 0:
                return jnp.concatenate([t[rows(b)], t[rows(b)]], axis=0)
            return t[(b - 1) * band:(b + 1) * band]

        s = jnp.concatenate(
            [jnp.where(mask_first if b % nblk == 0 else mask_rest,
                       lax.dot_general(q[rows(b)], window_rows(k, b), _CONTRACT_LAST,
                                       preferred_element_type=F32), NEG)
             for b in range(n_blocks)], axis=0)
        m = jnp.max(s, axis=1, keepdims=True)
        p = jnp.exp2((s - m) * (scale * _LOG2E))
        l = jnp.sum(p, axis=1, keepdims=True)
        p = p.astype(BF16)
        lse = m * scale + jnp.log(l)
        for b in range(n_blocks):
            o = jnp.dot(p[rows(b)], window_rows(v, b), preferred_element_type=F32) / l[rows(b)]
            r, n = divmod(b, nblk)
            dst = pl.ds(r + n * band * d, band, stride=d) if d > 1 else pl.ds(b * band, band)
            o_scr[g, dst, :] = o
            l_scr[g, dst, :] = jnp.broadcast_to(lse[rows(b)], (band, LANE))

    lses = [l_scr[g] for g in range(len(DIL_CONFIGS))]
    m = functools.reduce(jnp.maximum, lses)
    es = [jnp.exp(l - m) for l in lses]
    den = functools.reduce(lambda a, b: a + b, es)
    num = functools.reduce(lambda a, b: a + b, [(e / den) * o_scr[g] for g, e in enumerate(es)])
    o_ref[...] = (num * z_ref[...].astype(F32)).astype(BF16)


def _dilated(p_slots, offs):
    B, _, S, _ = p_slots.shape
    G = len(DIL_CONFIGS)
    hg = DIL_HEADS_PER_GROUP
    for window, d in DIL_CONFIGS:
        band = window // d
        assert band == LANE and S % (d * band) == 0

    def slab(base):
        return pl.BlockSpec((None, None, S, LANE), lambda b, h: (b, base + h, 0, 0))

    in_specs = ([slab(offs[f"{name}{g}"]) for name in ("qb", "kb", "vb") for g in range(G)]
                + [slab(offs["zb"])])
    return pl.pallas_call(
        _dilated_kernel,
        grid=(B, hg),
        in_specs=in_specs,
        out_specs=pl.BlockSpec((None, None, S, LANE), lambda b, h: (b, h, 0, 0)),
        out_shape=jax.ShapeDtypeStruct((B, hg, S, LANE), BF16),
        scratch_shapes=[pltpu.VMEM((G, S, LANE), F32)] * 2,
        compiler_params=pltpu.CompilerParams(
            dimension_semantics=("arbitrary", "arbitrary"),
            vmem_limit_bytes=VMEM_LIMIT_BYTES),
        name="dilated",
    )(*([p_slots] * (3 * G + 1)))


def _mem_kv_kernel(mem_ref, g_ref, w_ref, o_ref):
    h = _rms_norm(mem_ref[...], g_ref[...]).astype(BF16)
    o_ref[...] = jnp.dot(h, w_ref[...], preferred_element_type=F32).astype(BF16)


def _mem_kv(mem, g, w_bf):
    B, M, D = mem.shape
    width = w_bf.shape[1]
    return pl.pallas_call(
        _mem_kv_kernel,
        grid=(B,),
        in_specs=[
            pl.BlockSpec((None, M, D), lambda b: (b, 0, 0)),
            pl.BlockSpec((1, D), lambda b: (0, 0)),
            pl.BlockSpec((D, width), lambda b: (0, 0), pipeline_mode=pl.Buffered(1)),
        ],
        out_specs=pl.BlockSpec((None, M, width), lambda b: (b, 0, 0)),
        out_shape=jax.ShapeDtypeStruct((B, M, width), BF16),
        compiler_params=pltpu.CompilerParams(
            dimension_semantics=("arbitrary",),
            vmem_limit_bytes=VMEM_LIMIT_BYTES),
        name="mem_kv",
    )(mem, g.reshape(1, D), w_bf)


def _out_proj_kernel(x_ref, ya_ref, yb_ref, qm_ref, zm_ref, ga0, ga1, gb0, gb1, gm0, gm1, mkv_ref,
                     wa_ref, wb_ref, wm_ref, wo_ref, gf_ref, o_ref, *, final_norm):
    def slots(ref):
        return jnp.concatenate([ref[s] for s in range(ref.shape[0])], axis=1)

    def gate(r0, r1):
        return jnp.concatenate([slots(r0), slots(r1)], axis=1).astype(F32)

    merged = (gate(ga0, ga1) * jnp.dot(slots(ya_ref), wa_ref[...], preferred_element_type=F32)
              + gate(gb0, gb1) * jnp.dot(slots(yb_ref), wb_ref[...], preferred_element_type=F32))

    w_m = MEM_HEADS * MEM_HEAD_DIM
    scale = MEM_HEAD_DIM ** -0.5
    qm = slots(qm_ref)
    zm = slots(zm_ref)
    heads = [slice(h * MEM_HEAD_DIM, (h + 1) * MEM_HEAD_DIM) for h in range(MEM_HEADS)]
    s = [lax.dot_general(qm[:, c], mkv_ref[:, c], _CONTRACT_LAST, preferred_element_type=F32) for c in heads]
    m = [jnp.max(si, axis=1, keepdims=True) for si in s]
    p = [jnp.exp2((si - mi) * (scale * _LOG2E)) for si, mi in zip(s, m)]
    l = [jnp.sum(pi, axis=1, keepdims=True) for pi in p]
    pv = [jnp.dot(pi.astype(BF16), mkv_ref[:, w_m + c.start:w_m + c.stop], preferred_element_type=F32)
          for pi, c in zip(p, heads)]
    ym = jnp.concatenate([((pvi / li) * zm[:, c].astype(F32)).astype(BF16)
                          for pvi, li, c in zip(pv, l, heads)], axis=1)
    merged = merged + gate(gm0, gm1) * jnp.dot(ym, wm_ref[...], preferred_element_type=F32)
    y = x_ref[...] + jnp.dot(merged.astype(BF16), wo_ref[...], preferred_element_type=F32)
    o_ref[...] = _rms_norm(y, gf_ref[...]) if final_norm else y


def _out_proj(x, ya, yb, p_slots, offs, mkv, wa, wb, wm, wo, g_final, final_norm):
    B, S, D = x.shape
    tm = OUT_TM
    gslots = D // LANE // 2
    assert S % tm == 0 and offs["gates"] % gslots == 0 and gslots == MOBA_HEADS
    assert offs["qm"] % gslots == 0 and offs["zm"] % gslots == 0

    def pblock(slot0):
        blk = slot0 // gslots
        return pl.BlockSpec((None, gslots, tm, LANE), lambda b, s: (b, blk, s, 0))

    def resident(w):
        return pl.BlockSpec(w.shape, lambda b, s: (0,) * w.ndim, pipeline_mode=pl.Buffered(1))

    gate_specs = [pblock(offs["gates"] + i * gslots) for i in range(2 * N_BRANCH)]
    return pl.pallas_call(
        functools.partial(_out_proj_kernel, final_norm=final_norm),
        grid=(B, S // tm),
        in_specs=[
            pl.BlockSpec((None, tm, D), lambda b, s: (b, s, 0)),
            pl.BlockSpec((None, MOBA_HEADS, tm, LANE), lambda b, s: (b, 0, s, 0)),
            pl.BlockSpec((None, DIL_HEADS_PER_GROUP, tm, LANE), lambda b, s: (b, 0, s, 0)),
            pblock(offs["qm"]), pblock(offs["zm"]), *gate_specs,
            pl.BlockSpec((None,) + mkv.shape[1:], lambda b, s: (b, 0, 0)),
            resident(wa), resident(wb), resident(wm), resident(wo),
            pl.BlockSpec((1, D), lambda b, s: (0, 0)),
        ],
        out_specs=pl.BlockSpec((None, tm, D), lambda b, s: (b, s, 0)),
        out_shape=jax.ShapeDtypeStruct((B, S, D), F32),
        compiler_params=pltpu.CompilerParams(
            dimension_semantics=("arbitrary", "arbitrary"),
            vmem_limit_bytes=VMEM_LIMIT_BYTES),
        name="out_proj",
    )(x, ya, yb, *([p_slots] * (2 + 2 * N_BRANCH)), mkv, wa, wb, wm, wo, g_final.reshape(1, D))


def _rope_tables(S):
    half = HEAD_DIM // 2
    inv = ROPE_THETA ** (-jnp.arange(half, dtype=F32) / half)
    ang = jnp.arange(S, dtype=F32)[:, None] * inv[None, :]
    cos, sin = jnp.cos(ang), jnp.sin(ang)
    return jnp.concatenate([cos, cos], axis=1), jnp.concatenate([-sin, sin], axis=1)


def kernel(x, mem, norm_in_g, norm_mem_g, w_in, w_mem_kv, w_proj_a, w_proj_b, w_proj_m, w_out, norm_final_g):
    B, S, D = x.shape
    depth = w_in.shape[0]
    offs, n_slots = _slot_offsets(D)
    assert w_in.shape[2] == n_slots * LANE
    cos_t, sin_t = _rope_tables(S)
    for layer in range(depth):
        p_slots = _in_proj(x, norm_in_g[layer], w_in[layer], cos_t, sin_t)
        ya = _moba(p_slots, offs)
        yb = _dilated(p_slots, offs)
        mkv = _mem_kv(mem, norm_mem_g, w_mem_kv[layer].astype(BF16))
        x = _out_proj(x, ya, yb, p_slots, offs, mkv,
                      w_proj_a[layer].astype(BF16), w_proj_b[layer].astype(BF16),
                      w_proj_m[layer].astype(BF16), w_out[layer].astype(BF16),
                      norm_final_g, final_norm=(layer == depth - 1))
    return x
```

```python
import functools

import jax
import jax.numpy as jnp
from jax import lax
from jax.experimental import pallas as pl
from jax.experimental.pallas import tpu as pltpu

F32 = jnp.float32
BF16 = jnp.bfloat16

HEAD_DIM = 128
MOBA_HEADS = 8
MOBA_BLOCK = 256
MOBA_TOPK = 3
DIL_CONFIGS = ((128, 1), (512, 4), (2048, 16))
DIL_HEADS_PER_GROUP = 4
MEM_HEADS = 4
MEM_HEAD_DIM = 256
ROPE_THETA = 10000.0
RMS_EPS = 1e-6
NEG = -1e30
N_BRANCH = 3

LANE = 128
F32_SUBLANES = 8
BF16_SUBLANES = 16
V7X_VMEM_BYTES = 64 * 1024 * 1024
VMEM_LIMIT_BYTES = V7X_VMEM_BYTES - 8 * 1024 * 1024

IN_TN = 512
IN_CM = 256
IN_STAGE_BUFS = 8
OUT_TM = 256

_CONTRACT_LAST = (((1,), (1,)), ((), ()))
_LOG2E = 1.4426950408889634


def _segments(d_model):
    w_a = MOBA_HEADS * HEAD_DIM
    w_g = DIL_HEADS_PER_GROUP * HEAD_DIM
    w_m = MEM_HEADS * MEM_HEAD_DIM

    def groups(name, kind):
        return tuple((f"{name}{g}", w_g, kind, d) for g, (_, d) in enumerate(DIL_CONFIGS))

    return (("qa", w_a, "rope", 1), ("ka", w_a, "rope", 1), ("va", w_a, "ident", 1), ("za", w_a, "silu", 1),
            *groups("qb", "rope"), *groups("kb", "rope"), *groups("vb", "ident"), ("zb", w_g, "silu", 1),
            ("qm", w_m, "ident", 1), ("zm", w_m, "silu", 1), ("gates", N_BRANCH * d_model, "sigmoid", 1))


def _slot_offsets(d_model):
    offs, col = {}, 0
    for name, width, _, _ in _segments(d_model):
        offs[name] = col // LANE
        col += width
    return offs, col // LANE


def _rms_norm(xf, g):
    ms = jnp.mean(xf * xf, axis=-1, keepdims=True)
    return (xf * lax.rsqrt(ms + RMS_EPS)) * g


def _in_proj_kernel(x_hbm, g_ref, w_ref, cos_ref, sin_ref, o_ref, x_buf, h_ref, stage_ref, x_sem, *,
                    tile_kinds):
    i = pl.program_id(0)
    j = pl.program_id(1)
    n_slots = o_ref.shape[0]
    S = h_ref.shape[0]

    def x_copy(b):
        return pltpu.make_async_copy(x_hbm.at[b], x_buf, x_sem)

    @pl.when((i == 0) & (j == 0))
    def _():
        x_copy(0).start()

    @pl.when(j == 0)
    def _():
        x_copy(i).wait()
        h_ref[...] = _rms_norm(x_buf[...], g_ref[...]).astype(BF16)

    @pl.when((j == 1) & (i + 1 < pl.num_programs(0)))
    def _():
        x_copy(i + 1).start()

    def in_tiles(ranges):
        pred = None
        for lo, hi in ranges:
            p = (j >= lo) & (j < hi)
            pred = p if pred is None else (pred | p)
        return pred

    def run(epilogue, d):
        w = w_ref[...].astype(BF16)
        n = IN_CM // d
        for c in range(S // IN_CM):
            rows = pl.ds(c * IN_CM, IN_CM)
            acc = jnp.dot(h_ref[rows, :], w, preferred_element_type=F32)
            for s in range(n_slots):
                e = epilogue(acc[:, s * LANE:(s + 1) * LANE], rows)
                if d == 1:
                    o_ref[s, rows, :] = e.astype(BF16)
                    continue
                pair = 2 * ((c * n_slots + s) % (stage_ref.shape[0] // 2))
                stage, stage2 = stage_ref.at[pair], stage_ref.at[pair + 1]
                stage[...] = e
                if d == 16:
                    quarter = IN_CM // 4
                    for r4 in range(4):
                        stage2[pl.ds(r4 * quarter, quarter), :] = stage[pl.ds(r4, quarter, stride=4), :]
                    parts = {r4 + 4 * r2: stage2[pl.ds(r4 * quarter + r2, n, stride=4), :]
                             for r4 in range(4) for r2 in range(4)}
                else:
                    parts = {r: stage[pl.ds(r, n, stride=d), :] for r in range(d)}
                for r, part in parts.items():
                    o_ref[s, pl.ds(r * (S // d) + c * n, n), :] = part.astype(BF16)

    epilogues = {
        "rope": lambda a, rows: a * cos_ref[rows, :] + pltpu.roll(a, HEAD_DIM // 2, 1) * sin_ref[rows, :],
        "ident": lambda a, rows: a,
        "silu": lambda a, rows: a * jax.nn.sigmoid(a),
        "sigmoid": lambda a, rows: jax.nn.sigmoid(a),
    }
    for (kind, d), ranges in tile_kinds.items():
        pl.when(in_tiles(ranges))(functools.partial(run, epilogues[kind], d))


def _in_proj(x, g, w, cos_t, sin_t):
    B, S, D = x.shape
    width = w.shape[1]
    tn = IN_TN
    assert S % IN_CM == 0 and width % tn == 0 and tn % LANE == 0
    tile_kinds = {}
    col = 0
    for _, seg_w, kind, d in _segments(D):
        assert col % tn == 0 and seg_w % tn == 0 and (IN_CM // d) % BF16_SUBLANES == 0 and d in (1, 4, 16)
        tile_kinds.setdefault((kind, d), []).append((col // tn, (col + seg_w) // tn))
        col += seg_w
    assert col == width

    def table_spec():
        return pl.BlockSpec((S, LANE), lambda i, j: (0, 0), pipeline_mode=pl.Buffered(1))

    return pl.pallas_call(
        functools.partial(_in_proj_kernel, tile_kinds=tile_kinds),
        grid=(B, width // tn),
        in_specs=[
            pl.BlockSpec(memory_space=pltpu.HBM),
            pl.BlockSpec((1, D), lambda i, j: (0, 0)),
            pl.BlockSpec((D, tn), lambda i, j: (0, j)),
            table_spec(), table_spec(),
        ],
        out_specs=pl.BlockSpec((None, tn // LANE, S, LANE), lambda i, j: (i, j, 0, 0)),
        out_shape=jax.ShapeDtypeStruct((B, width // LANE, S, LANE), BF16),
        scratch_shapes=[pltpu.VMEM((S, D), F32), pltpu.VMEM((S, D), BF16),
                        pltpu.VMEM((IN_STAGE_BUFS, IN_CM, LANE), F32), pltpu.SemaphoreType.DMA(())],
        compiler_params=pltpu.CompilerParams(
            dimension_semantics=("arbitrary", "arbitrary"),
            vmem_limit_bytes=VMEM_LIMIT_BYTES),
        name="in_proj",
    )(x, g.reshape(1, D), w, cos_t, sin_t)


def _reduce_rows(op, a):
    rows, n = a.shape
    chain = 16
    a = a.reshape(rows // (chain * F32_SUBLANES), chain, F32_SUBLANES, n)
    return op(op(op(a, axis=1), axis=0), axis=0, keepdims=True)


def _moba_kernel(q_ref, k_ref, v_ref, z_ref, o_ref):
    S = q_ref.shape[0]
    blk = MOBA_BLOCK
    nb = S // blk
    scale = HEAD_DIM ** -0.5
    k = k_ref[...]
    q_t = q_ref[...].T
    v_t = v_ref[...].T
    v_t = jnp.concatenate([v_t, jnp.ones((BF16_SUBLANES, S), BF16)], axis=0)

    kmean = jnp.mean(k.astype(F32).reshape(nb, blk, HEAD_DIM), axis=1)
    k_hi = kmean.astype(BF16).astype(F32)
    k_hl = jnp.concatenate([k_hi, kmean - k_hi], axis=0).astype(BF16)
    gate2 = jnp.dot(k_hl, q_t, preferred_element_type=F32)
    gate = gate2[:nb] + gate2[nb:]

    key_row = lax.broadcasted_iota(jnp.int32, (blk, blk), 0)
    query_col = lax.broadcasted_iota(jnp.int32, (blk, blk), 1)
    blk_row = lax.broadcasted_iota(jnp.int32, (nb, blk), 0)

    key_blk = lax.broadcasted_iota(jnp.int32, (S, LANE), 0) // blk
    onehot = (key_blk == lax.broadcasted_iota(jnp.int32, (S, LANE), 1)).astype(BF16)
    k_aug = jnp.concatenate([k, onehot], axis=1)

    def cols(i):
        return slice(i * blk, (i + 1) * blk)

    def scores(i):
        kv_len = (i + 1) * blk
        if i == 0:
            s = jnp.dot(k[:kv_len], q_t[:, cols(i)], preferred_element_type=F32)
            return jnp.where(key_row <= query_col, s, NEG)
        g = gate[:, cols(i)]
        rank = jnp.zeros((nb, blk), jnp.int32)
        for jp in range(i):
            c = g[jp:jp + 1, :]
            beats = (c > g) | ((c == g) & (blk_row > jp))
            rank = rank + beats.astype(jnp.int32)
        bias = jnp.where((rank >= MOBA_TOPK) & (blk_row < i), NEG, 0.0)
        bias = jnp.concatenate([bias, jnp.zeros((LANE - nb, blk), F32)], axis=0).astype(BF16)
        q_aug = jnp.concatenate([q_t[:, cols(i)], bias], axis=0)
        s = jnp.dot(k_aug[:kv_len], q_aug, preferred_element_type=F32)
        own = jnp.where(key_row <= query_col, s[i * blk:], NEG)
        return jnp.concatenate([s[:i * blk], own], axis=0)

    s_all = [scores(i) for i in range(nb)]
    m = [_reduce_rows(jnp.max, s) for s in s_all]
    p = [jnp.exp2((s - mi) * (scale * _LOG2E)) for s, mi in zip(s_all, m)]
    pv_t = [jnp.dot(v_t[:, :(i + 1) * blk], p[i].astype(BF16), preferred_element_type=F32)
            for i in range(nb)]
    for i in range(nb):
        out = (pv_t[i][:HEAD_DIM] / pv_t[i][HEAD_DIM:HEAD_DIM + 1]).T * z_ref[cols(i), :].astype(F32)
        o_ref[cols(i), :] = out.astype(BF16)


def _moba(p_slots, offs):
    B, _, S, _ = p_slots.shape
    assert S % MOBA_BLOCK == 0 and S // MOBA_BLOCK <= LANE

    def slab(base):
        return pl.BlockSpec((None, None, S, LANE), lambda b, h: (b, base + h, 0, 0))

    return pl.pallas_call(
        _moba_kernel,
        grid=(B, MOBA_HEADS),
        in_specs=[slab(offs["qa"]), slab(offs["ka"]), slab(offs["va"]), slab(offs["za"])],
        out_specs=pl.BlockSpec((None, None, S, LANE), lambda b, h: (b, h, 0, 0)),
        out_shape=jax.ShapeDtypeStruct((B, MOBA_HEADS, S, LANE), BF16),
        compiler_params=pltpu.CompilerParams(
            dimension_semantics=("arbitrary", "arbitrary"),
            vmem_limit_bytes=VMEM_LIMIT_BYTES),
        name="moba",
    )(p_slots, p_slots, p_slots, p_slots)


def _dilated_kernel(q0, q1, q2, k0, k1, k2, v0, v1, v2, z_ref, o_ref, o_scr, l_scr):
    S = z_ref.shape[0]
    scale = HEAD_DIM ** -0.5
    q_refs, k_refs, v_refs = (q0, q1, q2), (k0, k1, k2), (v0, v1, v2)

    band = LANE
    n_blocks = S // band
    row = lax.broadcasted_iota(jnp.int32, (band, 2 * band), 0)
    col = lax.broadcasted_iota(jnp.int32, (band, 2 * band), 1)
    mask_rest = (col >= row) & (col <= row + band)
    mask_first = mask_rest & (col >= band)
    ones = jnp.ones((2 * band, LANE), BF16)

    for g, (window, d) in enumerate(DIL_CONFIGS):
        assert window // d == band
        nblk = S // d // band
        q = q_refs[g][...]
        k = k_refs[g][...]
        v = v_refs[g][...]

        def rows(b):
            return slice(b * band, (b + 1) * band)

        def window_rows(t, b):
            if b % nblk == 0:
                return jnp.concatenate([t[rows(b)], t[rows(b)]], axis=0)
            return t[(b - 1) * band:(b + 1) * band]

        s = jnp.concatenate(
            [jnp.where(mask_first if b % nblk == 0 else mask_rest,
                       lax.dot_general(q[rows(b)], window_rows(k, b), _CONTRACT_LAST,
                                       preferred_element_type=F32), NEG)
             for b in range(n_blocks)], axis=0)
        m = jnp.max(s, axis=1, keepdims=True)
        p = jnp.exp2((s - m) * (scale * _LOG2E))
        p = p.astype(BF16)
        for b in range(n_blocks):
            pv_l = jnp.dot(p[rows(b)], jnp.concatenate([window_rows(v, b), ones], axis=1),
                           preferred_element_type=F32)
            l = pv_l[:, HEAD_DIM:]
            r, n = divmod(b, nblk)
            dst = pl.ds(r + n * band * d, band, stride=d) if d > 1 else pl.ds(b * band, band)
            o_scr[g, dst, :] = pv_l[:, :HEAD_DIM] / l
            l_scr[g, dst, :] = m[rows(b)] * scale + jnp.log(l)

    lses = [l_scr[g] for g in range(len(DIL_CONFIGS))]
    m = functools.reduce(jnp.maximum, lses)
    es = [jnp.exp(l - m) for l in lses]
    den = functools.reduce(lambda a, b: a + b, es)
    num = functools.reduce(lambda a, b: a + b, [(e / den) * o_scr[g] for g, e in enumerate(es)])
    o_ref[...] = (num * z_ref[...].astype(F32)).astype(BF16)


def _dilated(p_slots, offs):
    B, _, S, _ = p_slots.shape
    G = len(DIL_CONFIGS)
    hg = DIL_HEADS_PER_GROUP
    for window, d in DIL_CONFIGS:
        band = window // d
        assert band == LANE and S % (d * band) == 0

    def slab(base):
        return pl.BlockSpec((None, None, S, LANE), lambda b, h: (b, base + h, 0, 0))

    in_specs = ([slab(offs[f"{name}{g}"]) for name in ("qb", "kb", "vb") for g in range(G)]
                + [slab(offs["zb"])])
    return pl.pallas_call(
        _dilated_kernel,
        grid=(B, hg),
        in_specs=in_specs,
        out_specs=pl.BlockSpec((None, None, S, LANE), lambda b, h: (b, h, 0, 0)),
        out_shape=jax.ShapeDtypeStruct((B, hg, S, LANE), BF16),
        scratch_shapes=[pltpu.VMEM((G, S, LANE), F32)] * 2,
        compiler_params=pltpu.CompilerParams(
            dimension_semantics=("arbitrary", "arbitrary"),
            vmem_limit_bytes=VMEM_LIMIT_BYTES),
        name="dilated",
    )(*([p_slots] * (3 * G + 1)))


def _mem_kv_kernel(mem_ref, g_ref, w_ref, o_ref):
    h = _rms_norm(mem_ref[...], g_ref[...]).astype(BF16)
    o_ref[...] = jnp.dot(h, w_ref[...], preferred_element_type=F32).astype(BF16)


def _mem_kv(mem, g, w_bf):
    B, M, D = mem.shape
    width = w_bf.shape[1]
    return pl.pallas_call(
        _mem_kv_kernel,
        grid=(B,),
        in_specs=[
            pl.BlockSpec((None, M, D), lambda b: (b, 0, 0)),
            pl.BlockSpec((1, D), lambda b: (0, 0)),
            pl.BlockSpec((D, width), lambda b: (0, 0), pipeline_mode=pl.Buffered(1)),
        ],
        out_specs=pl.BlockSpec((None, M, width), lambda b: (b, 0, 0)),
        out_shape=jax.ShapeDtypeStruct((B, M, width), BF16),
        compiler_params=pltpu.CompilerParams(
            dimension_semantics=("arbitrary",),
            vmem_limit_bytes=VMEM_LIMIT_BYTES),
        name="mem_kv",
    )(mem, g.reshape(1, D), w_bf)


def _out_proj_kernel(x_ref, ya_ref, yb_ref, qm_ref, zm_ref, ga0, ga1, gb0, gb1, gm0, gm1, mkv_ref,
                     wa_ref, wb_ref, wm_ref, wo_ref, gf_ref, o_ref, *, final_norm):
    def slots(ref):
        return jnp.concatenate([ref[s] for s in range(ref.shape[0])], axis=1)

    def gate(r0, r1):
        return jnp.concatenate([slots(r0), slots(r1)], axis=1).astype(F32)

    merged = (gate(ga0, ga1) * jnp.dot(slots(ya_ref), wa_ref[...], preferred_element_type=F32)
              + gate(gb0, gb1) * jnp.dot(slots(yb_ref), wb_ref[...], preferred_element_type=F32))

    w_m = MEM_HEADS * MEM_HEAD_DIM
    scale = MEM_HEAD_DIM ** -0.5
    qm = slots(qm_ref)
    zm = slots(zm_ref)
    heads = [slice(h * MEM_HEAD_DIM, (h + 1) * MEM_HEAD_DIM) for h in range(MEM_HEADS)]
    s = [lax.dot_general(qm[:, c], mkv_ref[:, c], _CONTRACT_LAST, preferred_element_type=F32) for c in heads]
    m = [jnp.max(si, axis=1, keepdims=True) for si in s]
    p = [jnp.exp2((si - mi) * (scale * _LOG2E)) for si, mi in zip(s, m)]
    l = [jnp.sum(pi, axis=1, keepdims=True) for pi in p]
    pv = [jnp.dot(pi.astype(BF16), mkv_ref[:, w_m + c.start:w_m + c.stop], preferred_element_type=F32)
          for pi, c in zip(p, heads)]
    ym = jnp.concatenate([((pvi / li) * zm[:, c].astype(F32)).astype(BF16)
                          for pvi, li, c in zip(pv, l, heads)], axis=1)
    merged = merged + gate(gm0, gm1) * jnp.dot(ym, wm_ref[...], preferred_element_type=F32)
    y = x_ref[...] + jnp.dot(merged.astype(BF16), wo_ref[...], preferred_element_type=F32)
    o_ref[...] = _rms_norm(y, gf_ref[...]) if final_norm else y


def _out_proj(x, ya, yb, p_slots, offs, mkv, wa, wb, wm, wo, g_final, final_norm):
    B, S, D = x.shape
    tm = OUT_TM
    gslots = D // LANE // 2
    assert S % tm == 0 and offs["gates"] % gslots == 0 and gslots == MOBA_HEADS
    assert offs["qm"] % gslots == 0 and offs["zm"] % gslots == 0

    def pblock(slot0):
        blk = slot0 // gslots
        return pl.BlockSpec((None, gslots, tm, LANE), lambda b, s: (b, blk, s, 0))

    def resident(w):
        return pl.BlockSpec(w.shape, lambda b, s: (0,) * w.ndim, pipeline_mode=pl.Buffered(1))

    gate_specs = [pblock(offs["gates"] + i * gslots) for i in range(2 * N_BRANCH)]
    return pl.pallas_call(
        functools.partial(_out_proj_kernel, final_norm=final_norm),
        grid=(B, S // tm),
        in_specs=[
            pl.BlockSpec((None, tm, D), lambda b, s: (b, s, 0)),
            pl.BlockSpec((None, MOBA_HEADS, tm, LANE), lambda b, s: (b, 0, s, 0)),
            pl.BlockSpec((None, DIL_HEADS_PER_GROUP, tm, LANE), lambda b, s: (b, 0, s, 0)),
            pblock(offs["qm"]), pblock(offs["zm"]), *gate_specs,
            pl.BlockSpec((None,) + mkv.shape[1:], lambda b, s: (b, 0, 0)),
            resident(wa), resident(wb), resident(wm), resident(wo),
            pl.BlockSpec((1, D), lambda b, s: (0, 0)),
        ],
        out_specs=pl.BlockSpec((None, tm, D), lambda b, s: (b, s, 0)),
        out_shape=jax.ShapeDtypeStruct((B, S, D), F32),
        compiler_params=pltpu.CompilerParams(
            dimension_semantics=("arbitrary", "arbitrary"),
            vmem_limit_bytes=VMEM_LIMIT_BYTES),
        name="out_proj",
    )(x, ya, yb, *([p_slots] * (2 + 2 * N_BRANCH)), mkv, wa, wb, wm, wo, g_final.reshape(1, D))


def _rope_tables(S):
    half = HEAD_DIM // 2
    inv = ROPE_THETA ** (-jnp.arange(half, dtype=F32) / half)
    ang = jnp.arange(S, dtype=F32)[:, None] * inv[None, :]
    cos, sin = jnp.cos(ang), jnp.sin(ang)
    return jnp.concatenate([cos, cos], axis=1), jnp.concatenate([-sin, sin], axis=1)


def kernel(x, mem, norm_in_g, norm_mem_g, w_in, w_mem_kv, w_proj_a, w_proj_b, w_proj_m, w_out, norm_final_g):
    B, S, D = x.shape
    depth = w_in.shape[0]
    offs, n_slots = _slot_offsets(D)
    assert w_in.shape[2] == n_slots * LANE
    cos_t, sin_t = _rope_tables(S)
    for layer in range(depth):
        p_slots = _in_proj(x, norm_in_g[layer], w_in[layer], cos_t, sin_t)
        ya = _moba(p_slots, offs)
        yb = _dilated(p_slots, offs)
        mkv = _mem_kv(mem, norm_mem_g, w_mem_kv[layer].astype(BF16))
        x = _out_proj(x, ya, yb, p_slots, offs, mkv,
                      w_proj_a[layer].astype(BF16), w_proj_b[layer].astype(BF16),
                      w_proj_m[layer].astype(BF16), w_out[layer].astype(BF16),
                      norm_final_g, final_norm=(layer == depth - 1))
    return x
```

```python
import functools

import jax
import jax.numpy as jnp
from jax import lax
from jax.experimental import pallas as pl
from jax.experimental.pallas import tpu as pltpu

F32 = jnp.float32
BF16 = jnp.bfloat16

HEAD_DIM = 128
MOBA_HEADS = 8
MOBA_BLOCK = 256
MOBA_TOPK = 3
DIL_CONFIGS = ((128, 1), (512, 4), (2048, 16))
DIL_HEADS_PER_GROUP = 4
MEM_HEADS = 4
MEM_HEAD_DIM = 256
ROPE_THETA = 10000.0
RMS_EPS = 1e-6
NEG = -1e30
N_BRANCH = 3

LANE = 128
F32_SUBLANES = 8
BF16_SUBLANES = 16
V7X_VMEM_BYTES = 64 * 1024 * 1024
VMEM_LIMIT_BYTES = V7X_VMEM_BYTES - 8 * 1024 * 1024

IN_TN = 512
IN_CM = 256
IN_STAGE_BUFS = 8
OUT_TM = 256
OUT_W_ROWS = 256

_CONTRACT_LAST = (((1,), (1,)), ((), ()))
_LOG2E = 1.4426950408889634


def _segments(d_model):
    w_a = MOBA_HEADS * HEAD_DIM
    w_g = DIL_HEADS_PER_GROUP * HEAD_DIM
    w_m = MEM_HEADS * MEM_HEAD_DIM

    def groups(name, kind):
        return tuple((f"{name}{g}", w_g, kind, d) for g, (_, d) in enumerate(DIL_CONFIGS))

    return (("qa", w_a, "rope", 1), ("ka", w_a, "rope", 1), ("va", w_a, "ident", 1), ("za", w_a, "silu", 1),
            *groups("qb", "rope"), *groups("kb", "rope"), *groups("vb", "ident"), ("zb", w_g, "silu", 1),
            ("qm", w_m, "ident", 1), ("zm", w_m, "silu", 1), ("gates", N_BRANCH * d_model, "sigmoid", 1))


def _slot_offsets(d_model):
    offs, col = {}, 0
    for name, width, _, _ in _segments(d_model):
        offs[name] = col // LANE
        col += width
    return offs, col // LANE


def _sigmoid(a):
    return 0.5 * jnp.tanh(0.5 * a) + 0.5


def _rms_norm(xf, g):
    ms = jnp.mean(xf * xf, axis=-1, keepdims=True)
    return (xf * lax.rsqrt(ms + RMS_EPS)) * g


def _in_proj_kernel(x_hbm, g_ref, w_ref, cos_ref, sin_ref, o_ref, x_buf, h_ref, stage_ref, x_sem, *,
                    tile_kinds):
    i = pl.program_id(0)
    j = pl.program_id(1)
    n_slots = o_ref.shape[0]
    S = h_ref.shape[0]

    def x_copy(b):
        return pltpu.make_async_copy(x_hbm.at[b], x_buf, x_sem)

    @pl.when((i == 0) & (j == 0))
    def _():
        x_copy(0).start()

    @pl.when(j == 0)
    def _():
        x_copy(i).wait()
        h_ref[...] = _rms_norm(x_buf[...], g_ref[...]).astype(BF16)

    @pl.when((j == 1) & (i + 1 < pl.num_programs(0)))
    def _():
        x_copy(i + 1).start()

    def in_tiles(ranges):
        pred = None
        for lo, hi in ranges:
            p = (j >= lo) & (j < hi)
            pred = p if pred is None else (pred | p)
        return pred

    def run(epilogue, d):
        w = w_ref[...].astype(BF16)
        n = IN_CM // d
        for c in range(S // IN_CM):
            rows = pl.ds(c * IN_CM, IN_CM)
            acc = jnp.dot(h_ref[rows, :], w, preferred_element_type=F32)
            for s in range(n_slots):
                e = epilogue(acc[:, s * LANE:(s + 1) * LANE], rows)
                if d == 1:
                    o_ref[s, rows, :] = e.astype(BF16)
                    continue
                pair = 2 * ((c * n_slots + s) % (stage_ref.shape[0] // 2))
                stage, stage2 = stage_ref.at[pair], stage_ref.at[pair + 1]
                stage[...] = e
                if d == 16:
                    quarter = IN_CM // 4
                    for r4 in range(4):
                        stage2[pl.ds(r4 * quarter, quarter), :] = stage[pl.ds(r4, quarter, stride=4), :]
                    parts = {r4 + 4 * r2: stage2[pl.ds(r4 * quarter + r2, n, stride=4), :]
                             for r4 in range(4) for r2 in range(4)}
                else:
                    parts = {r: stage[pl.ds(r, n, stride=d), :] for r in range(d)}
                for r, part in parts.items():
                    o_ref[s, pl.ds(r * (S // d) + c * n, n), :] = part.astype(BF16)

    epilogues = {
        "rope": lambda a, rows: a * cos_ref[rows, :] + pltpu.roll(a, HEAD_DIM // 2, 1) * sin_ref[rows, :],
        "ident": lambda a, rows: a,
        "silu": lambda a, rows: a * _sigmoid(a),
        "sigmoid": lambda a, rows: _sigmoid(a),
    }
    for (kind, d), ranges in tile_kinds.items():
        pl.when(in_tiles(ranges))(functools.partial(run, epilogues[kind], d))


def _in_proj(x, g, w, cos_t, sin_t):
    B, S, D = x.shape
    width = w.shape[1]
    tn = IN_TN
    assert S % IN_CM == 0 and width % tn == 0 and tn % LANE == 0
    tile_kinds = {}
    col = 0
    for _, seg_w, kind, d in _segments(D):
        assert col % tn == 0 and seg_w % tn == 0 and (IN_CM // d) % BF16_SUBLANES == 0 and d in (1, 4, 16)
        tile_kinds.setdefault((kind, d), []).append((col // tn, (col + seg_w) // tn))
        col += seg_w
    assert col == width

    def table_spec():
        return pl.BlockSpec((S, LANE), lambda i, j: (0, 0), pipeline_mode=pl.Buffered(1))

    return pl.pallas_call(
        functools.partial(_in_proj_kernel, tile_kinds=tile_kinds),
        grid=(B, width // tn),
        in_specs=[
            pl.BlockSpec(memory_space=pltpu.HBM),
            pl.BlockSpec((1, D), lambda i, j: (0, 0)),
            pl.BlockSpec((D, tn), lambda i, j: (0, j)),
            table_spec(), table_spec(),
        ],
        out_specs=pl.BlockSpec((None, tn // LANE, S, LANE), lambda i, j: (i, j, 0, 0)),
        out_shape=jax.ShapeDtypeStruct((B, width // LANE, S, LANE), BF16),
        scratch_shapes=[pltpu.VMEM((S, D), F32), pltpu.VMEM((S, D), BF16),
                        pltpu.VMEM((IN_STAGE_BUFS, IN_CM, LANE), F32), pltpu.SemaphoreType.DMA(())],
        compiler_params=pltpu.CompilerParams(
            dimension_semantics=("arbitrary", "arbitrary"),
            vmem_limit_bytes=VMEM_LIMIT_BYTES),
        name="in_proj",
    )(x, g.reshape(1, D), w, cos_t, sin_t)


def _reduce_rows(op, a):
    rows, n = a.shape
    chain = 16
    a = a.reshape(rows // (chain * F32_SUBLANES), chain, F32_SUBLANES, n)
    return op(op(op(a, axis=1), axis=0), axis=0, keepdims=True)


def _moba_kernel(q_ref, k_ref, v_ref, z_ref, o_ref):
    S = q_ref.shape[0]
    blk = MOBA_BLOCK
    nb = S // blk
    scale = HEAD_DIM ** -0.5
    k = k_ref[...]
    q_t = q_ref[...].T
    v_t = v_ref[...].T
    v_t = jnp.concatenate([v_t, jnp.ones((BF16_SUBLANES, S), BF16)], axis=0)

    kmean = jnp.mean(k.astype(F32).reshape(nb, blk, HEAD_DIM), axis=1)
    k_hi = kmean.astype(BF16).astype(F32)
    k_hl = jnp.concatenate([k_hi, kmean - k_hi], axis=0).astype(BF16)
    gate2 = jnp.dot(k_hl, q_t, preferred_element_type=F32)
    gate = gate2[:nb] + gate2[nb:]

    key_row = lax.broadcasted_iota(jnp.int32, (blk, blk), 0)
    query_col = lax.broadcasted_iota(jnp.int32, (blk, blk), 1)
    blk_row = lax.broadcasted_iota(jnp.int32, (nb, blk), 0)

    key_blk = lax.broadcasted_iota(jnp.int32, (S, LANE), 0) // blk
    onehot = (key_blk == lax.broadcasted_iota(jnp.int32, (S, LANE), 1)).astype(BF16)
    k_aug = jnp.concatenate([k, onehot], axis=1)

    def cols(i):
        return slice(i * blk, (i + 1) * blk)

    def scores(i):
        kv_len = (i + 1) * blk
        if i == 0:
            s = jnp.dot(k[:kv_len], q_t[:, cols(i)], preferred_element_type=F32)
            return jnp.where(key_row <= query_col, s, NEG)
        g = gate[:, cols(i)]
        rank = jnp.zeros((nb, blk), jnp.int32)
        for jp in range(i):
            c = g[jp:jp + 1, :]
            beats = (c > g) | ((c == g) & (blk_row > jp))
            rank = rank + beats.astype(jnp.int32)
        bias = jnp.where((rank >= MOBA_TOPK) & (blk_row < i), NEG, 0.0)
        bias = jnp.concatenate([bias, jnp.zeros((LANE - nb, blk), F32)], axis=0).astype(BF16)
        q_aug = jnp.concatenate([q_t[:, cols(i)], bias], axis=0)
        s = jnp.dot(k_aug[:kv_len], q_aug, preferred_element_type=F32)
        own = jnp.where(key_row <= query_col, s[i * blk:], NEG)
        return jnp.concatenate([s[:i * blk], own], axis=0)

    s_all = [scores(i) for i in range(nb)]
    m = [_reduce_rows(jnp.max, s) for s in s_all]
    p = [jnp.exp2((s - mi) * (scale * _LOG2E)) for s, mi in zip(s_all, m)]
    pv_t = [jnp.dot(v_t[:, :(i + 1) * blk], p[i].astype(BF16), preferred_element_type=F32)
            for i in range(nb)]
    for i in range(nb):
        out = (pv_t[i][:HEAD_DIM] / pv_t[i][HEAD_DIM:HEAD_DIM + 1]).T * z_ref[cols(i), :].astype(F32)
        o_ref[cols(i), :] = out.astype(BF16)


def _moba(p_slots, offs):
    B, _, S, _ = p_slots.shape
    assert S % MOBA_BLOCK == 0 and S // MOBA_BLOCK <= LANE

    def slab(base):
        return pl.BlockSpec((None, None, S, LANE), lambda b, h: (b, base + h, 0, 0))

    return pl.pallas_call(
        _moba_kernel,
        grid=(B, MOBA_HEADS),
        in_specs=[slab(offs["qa"]), slab(offs["ka"]), slab(offs["va"]), slab(offs["za"])],
        out_specs=pl.BlockSpec((None, None, S, LANE), lambda b, h: (b, h, 0, 0)),
        out_shape=jax.ShapeDtypeStruct((B, MOBA_HEADS, S, LANE), BF16),
        compiler_params=pltpu.CompilerParams(
            dimension_semantics=("arbitrary", "arbitrary"),
            vmem_limit_bytes=VMEM_LIMIT_BYTES),
        name="moba",
    )(p_slots, p_slots, p_slots, p_slots)


def _dilated_kernel(q0, q1, q2, k0, k1, k2, v0, v1, v2, z_ref, o_ref, o_scr, l_scr):
    S = z_ref.shape[0]
    scale = HEAD_DIM ** -0.5
    q_refs, k_refs, v_refs = (q0, q1, q2), (k0, k1, k2), (v0, v1, v2)

    band = LANE
    n_blocks = S // band
    row = lax.broadcasted_iota(jnp.int32, (band, 2 * band), 0)
    col = lax.broadcasted_iota(jnp.int32, (band, 2 * band), 1)
    mask_rest = (col >= row) & (col <= row + band)
    mask_first = mask_rest & (col >= band)
    ones = jnp.ones((2 * band, LANE), BF16)

    for g, (window, d) in enumerate(DIL_CONFIGS):
        assert window // d == band
        nblk = S // d // band
        q = q_refs[g][...]
        k = k_refs[g][...]
        v = v_refs[g][...]

        def rows(b):
            return slice(b * band, (b + 1) * band)

        def window_rows(t, b):
            if b % nblk == 0:
                return jnp.concatenate([t[rows(b)], t[rows(b)]], axis=0)
            return t[(b - 1) * band:(b + 1) * band]

        s = jnp.concatenate(
            [jnp.where(mask_first if b % nblk == 0 else mask_rest,
                       lax.dot_general(q[rows(b)], window_rows(k, b), _CONTRACT_LAST,
                                       preferred_element_type=F32), NEG)
             for b in range(n_blocks)], axis=0)
        m = jnp.max(s, axis=1, keepdims=True)
        p = jnp.exp2((s - m) * (scale * _LOG2E))
        p = p.astype(BF16)
        for b in range(n_blocks):
            pv_l = jnp.dot(p[rows(b)], jnp.concatenate([window_rows(v, b), ones], axis=1),
                           preferred_element_type=F32)
            l = pv_l[:, HEAD_DIM:]
            r, n = divmod(b, nblk)
            dst = pl.ds(r + n * band * d, band, stride=d) if d > 1 else pl.ds(b * band, band)
            o_scr[g, dst, :] = pv_l[:, :HEAD_DIM] / l
            l_scr[g, dst, :] = m[rows(b)] * scale + jnp.log(l)

    lses = [l_scr[g] for g in range(len(DIL_CONFIGS))]
    m = functools.reduce(jnp.maximum, lses)
    es = [jnp.exp(l - m) for l in lses]
    den = functools.reduce(lambda a, b: a + b, es)
    num = functools.reduce(lambda a, b: a + b, [(e / den) * o_scr[g] for g, e in enumerate(es)])
    o_ref[...] = (num * z_ref[...].astype(F32)).astype(BF16)


def _dilated(p_slots, offs):
    B, _, S, _ = p_slots.shape
    G = len(DIL_CONFIGS)
    hg = DIL_HEADS_PER_GROUP
    for window, d in DIL_CONFIGS:
        band = window // d
        assert band == LANE and S % (d * band) == 0

    def slab(base):
        return pl.BlockSpec((None, None, S, LANE), lambda b, h: (b, base + h, 0, 0))

    in_specs = ([slab(offs[f"{name}{g}"]) for name in ("qb", "kb", "vb") for g in range(G)]
                + [slab(offs["zb"])])
    return pl.pallas_call(
        _dilated_kernel,
        grid=(B, hg),
        in_specs=in_specs,
        out_specs=pl.BlockSpec((None, None, S, LANE), lambda b, h: (b, h, 0, 0)),
        out_shape=jax.ShapeDtypeStruct((B, hg, S, LANE), BF16),
        scratch_shapes=[pltpu.VMEM((G, S, LANE), F32)] * 2,
        compiler_params=pltpu.CompilerParams(
            dimension_semantics=("arbitrary", "arbitrary"),
            vmem_limit_bytes=VMEM_LIMIT_BYTES),
        name="dilated",
    )(*([p_slots] * (3 * G + 1)))


def _mem_kv_kernel(mem_ref, g_ref, w_ref, o_ref, wb_ref):
    @pl.when(pl.program_id(0) == 0)
    def _():
        wb_ref[...] = w_ref[...].astype(BF16)

    h = _rms_norm(mem_ref[...], g_ref[...]).astype(BF16)
    o_ref[...] = jnp.dot(h, wb_ref[...], preferred_element_type=F32).astype(BF16)


def _mem_kv(mem, g, w):
    B, M, D = mem.shape
    width = w.shape[1]
    return pl.pallas_call(
        _mem_kv_kernel,
        grid=(B,),
        in_specs=[
            pl.BlockSpec((None, M, D), lambda b: (b, 0, 0)),
            pl.BlockSpec((1, D), lambda b: (0, 0)),
            pl.BlockSpec((D, width), lambda b: (0, 0), pipeline_mode=pl.Buffered(1)),
        ],
        out_specs=pl.BlockSpec((None, M, width), lambda b: (b, 0, 0)),
        out_shape=jax.ShapeDtypeStruct((B, M, width), BF16),
        scratch_shapes=[pltpu.VMEM((D, width), BF16)],
        compiler_params=pltpu.CompilerParams(
            dimension_semantics=("arbitrary",),
            vmem_limit_bytes=VMEM_LIMIT_BYTES),
        name="mem_kv",
    )(mem, g.reshape(1, D), w)


def _load_weights_bf16(pairs, stage_ref, sems):
    rows = stage_ref.shape[1]
    chunks = [(src, dst, r0) for src, dst in pairs for r0 in range(0, src.shape[0], rows)]

    def copy(n):
        src, _, r0 = chunks[n]
        return pltpu.make_async_copy(src.at[pl.ds(r0, rows)], stage_ref.at[n % 2], sems.at[n % 2])

    copy(0).start()
    for n, (_, dst, r0) in enumerate(chunks):
        if n + 1 < len(chunks):
            copy(n + 1).start()
        copy(n).wait()
        dst[pl.ds(r0, rows), :] = stage_ref[n % 2].astype(BF16)


def _out_proj_kernel(x_ref, ya_ref, yb_ref, qm_ref, zm_ref, ga0, ga1, gb0, gb1, gm0, gm1, mkv_ref,
                     wa_hbm, wb_hbm, wm_hbm, wo_hbm, gf_ref, o_ref,
                     wa_ref, wb_ref, wm_ref, wo_ref, stage_ref, w_sems, *, final_norm):
    @pl.when((pl.program_id(0) == 0) & (pl.program_id(1) == 0))
    def _():
        _load_weights_bf16(((wa_hbm, wa_ref), (wb_hbm, wb_ref), (wm_hbm, wm_ref), (wo_hbm, wo_ref)),
                           stage_ref, w_sems)

    def slots(ref):
        return jnp.concatenate([ref[s] for s in range(ref.shape[0])], axis=1)

    def gate(r0, r1):
        return jnp.concatenate([slots(r0), slots(r1)], axis=1).astype(F32)

    merged = (gate(ga0, ga1) * jnp.dot(slots(ya_ref), wa_ref[...], preferred_element_type=F32)
              + gate(gb0, gb1) * jnp.dot(slots(yb_ref), wb_ref[...], preferred_element_type=F32))

    w_m = MEM_HEADS * MEM_HEAD_DIM
    scale = MEM_HEAD_DIM ** -0.5
    qm = slots(qm_ref)
    zm = slots(zm_ref)
    heads = [slice(h * MEM_HEAD_DIM, (h + 1) * MEM_HEAD_DIM) for h in range(MEM_HEADS)]
    s = [lax.dot_general(qm[:, c], mkv_ref[:, c], _CONTRACT_LAST, preferred_element_type=F32) for c in heads]
    m = [jnp.max(si, axis=1, keepdims=True) for si in s]
    p = [jnp.exp2((si - mi) * (scale * _LOG2E)) for si, mi in zip(s, m)]
    l = [jnp.sum(pi, axis=1, keepdims=True) for pi in p]
    pv = [jnp.dot(pi.astype(BF16), mkv_ref[:, w_m + c.start:w_m + c.stop], preferred_element_type=F32)
          for pi, c in zip(p, heads)]
    ym = jnp.concatenate([((pvi / li) * zm[:, c].astype(F32)).astype(BF16)
                          for pvi, li, c in zip(pv, l, heads)], axis=1)
    merged = merged + gate(gm0, gm1) * jnp.dot(ym, wm_ref[...], preferred_element_type=F32)
    y = x_ref[...] + jnp.dot(merged.astype(BF16), wo_ref[...], preferred_element_type=F32)
    o_ref[...] = _rms_norm(y, gf_ref[...]) if final_norm else y


def _out_proj(x, ya, yb, p_slots, offs, mkv, wa, wb, wm, wo, g_final, final_norm):
    B, S, D = x.shape
    tm = OUT_TM
    gslots = D // LANE // 2
    assert S % tm == 0 and offs["gates"] % gslots == 0 and gslots == MOBA_HEADS
    assert offs["qm"] % gslots == 0 and offs["zm"] % gslots == 0

    def pblock(slot0):
        blk = slot0 // gslots
        return pl.BlockSpec((None, gslots, tm, LANE), lambda b, s: (b, blk, s, 0))

    weights = (wa, wb, wm, wo)
    assert all(w.shape[1] == D and w.shape[0] % OUT_W_ROWS == 0 for w in weights)
    gate_specs = [pblock(offs["gates"] + i * gslots) for i in range(2 * N_BRANCH)]
    return pl.pallas_call(
        functools.partial(_out_proj_kernel, final_norm=final_norm),
        grid=(B, S // tm),
        in_specs=[
            pl.BlockSpec((None, tm, D), lambda b, s: (b, s, 0)),
            pl.BlockSpec((None, MOBA_HEADS, tm, LANE), lambda b, s: (b, 0, s, 0)),
            pl.BlockSpec((None, DIL_HEADS_PER_GROUP, tm, LANE), lambda b, s: (b, 0, s, 0)),
            pblock(offs["qm"]), pblock(offs["zm"]), *gate_specs,
            pl.BlockSpec((None,) + mkv.shape[1:], lambda b, s: (b, 0, 0)),
            *([pl.BlockSpec(memory_space=pltpu.HBM)] * len(weights)),
            pl.BlockSpec((1, D), lambda b, s: (0, 0)),
        ],
        out_specs=pl.BlockSpec((None, tm, D), lambda b, s: (b, s, 0)),
        out_shape=jax.ShapeDtypeStruct((B, S, D), F32),
        scratch_shapes=[pltpu.VMEM(w.shape, BF16) for w in weights]
        + [pltpu.VMEM((2, OUT_W_ROWS, D), F32), pltpu.SemaphoreType.DMA((2,))],
        compiler_params=pltpu.CompilerParams(
            dimension_semantics=("arbitrary", "arbitrary"),
            vmem_limit_bytes=VMEM_LIMIT_BYTES),
        name="out_proj",
    )(x, ya, yb, *([p_slots] * (2 + 2 * N_BRANCH)), mkv, wa, wb, wm, wo, g_final.reshape(1, D))


def _rope_tables(S):
    half = HEAD_DIM // 2
    inv = ROPE_THETA ** (-jnp.arange(half, dtype=F32) / half)
    ang = jnp.arange(S, dtype=F32)[:, None] * inv[None, :]
    cos, sin = jnp.cos(ang), jnp.sin(ang)
    return jnp.concatenate([cos, cos], axis=1), jnp.concatenate([-sin, sin], axis=1)


def kernel(x, mem, norm_in_g, norm_mem_g, w_in, w_mem_kv, w_proj_a, w_proj_b, w_proj_m, w_out, norm_final_g):
    B, S, D = x.shape
    depth = w_in.shape[0]
    offs, n_slots = _slot_offsets(D)
    assert w_in.shape[2] == n_slots * LANE
    cos_t, sin_t = _rope_tables(S)
    for layer in range(depth):
        p_slots = _in_proj(x, norm_in_g[layer], w_in[layer], cos_t, sin_t)
        ya = _moba(p_slots, offs)
        yb = _dilated(p_slots, offs)
        mkv = _mem_kv(mem, norm_mem_g, w_mem_kv[layer])
        x = _out_proj(x, ya, yb, p_slots, offs, mkv,
                      w_proj_a[layer], w_proj_b[layer], w_proj_m[layer], w_out[layer],
                      norm_final_g, final_norm=(layer == depth - 1))
    return x
```

```python
import functools

import jax
import jax.numpy as jnp
from jax import lax
from jax.experimental import pallas as pl
from jax.experimental.pallas import tpu as pltpu

F32 = jnp.float32
BF16 = jnp.bfloat16

HEAD_DIM = 128
MOBA_HEADS = 8
MOBA_BLOCK = 256
MOBA_TOPK = 3
DIL_CONFIGS = ((128, 1), (512, 4), (2048, 16))
DIL_HEADS_PER_GROUP = 4
MEM_HEADS = 4
MEM_HEAD_DIM = 256
ROPE_THETA = 10000.0
RMS_EPS = 1e-6
NEG = -1e30
N_BRANCH = 3

LANE = 128
F32_SUBLANES = 8
BF16_SUBLANES = 16
V7X_VMEM_BYTES = 64 * 1024 * 1024
VMEM_LIMIT_BYTES = V7X_VMEM_BYTES - 8 * 1024 * 1024

IN_TN = 512
IN_CM = 256
IN_STAGE_BUFS = 8
OUT_TM = 256
OUT_W_ROWS = 256

_CONTRACT_LAST = (((1,), (1,)), ((), ()))
_LOG2E = 1.4426950408889634


def _segments(d_model):
    w_a = MOBA_HEADS * HEAD_DIM
    w_g = DIL_HEADS_PER_GROUP * HEAD_DIM
    w_m = MEM_HEADS * MEM_HEAD_DIM

    def groups(name, kind):
        return tuple((f"{name}{g}", w_g, kind, d) for g, (_, d) in enumerate(DIL_CONFIGS))

    return (("qa", w_a, "rope", 1), ("ka", w_a, "rope", 1), ("va", w_a, "ident", 1), ("za", w_a, "silu", 1),
            *groups("qb", "rope"), *groups("kb", "rope"), *groups("vb", "ident"), ("zb", w_g, "silu", 1),
            ("qm", w_m, "ident", 1), ("zm", w_m, "silu", 1), ("gates", N_BRANCH * d_model, "sigmoid", 1))


def _slot_offsets(d_model):
    offs, col = {}, 0
    for name, width, _, _ in _segments(d_model):
        offs[name] = col // LANE
        col += width
    return offs, col // LANE


def _rms_norm(xf, g):
    ms = jnp.mean(xf * xf, axis=-1, keepdims=True)
    return (xf * lax.rsqrt(ms + RMS_EPS)) * g


def _in_proj_kernel(x_hbm, g_ref, w_ref, cos_ref, sin_ref, o_ref, x_buf, h_ref, stage_ref, x_sem, *,
                    tile_kinds):
    i = pl.program_id(0)
    j = pl.program_id(1)
    n_slots = o_ref.shape[0]
    S = h_ref.shape[0]

    def x_copy(b):
        return pltpu.make_async_copy(x_hbm.at[b], x_buf, x_sem)

    @pl.when((i == 0) & (j == 0))
    def _():
        x_copy(0).start()

    @pl.when(j == 0)
    def _():
        x_copy(i).wait()
        h_ref[...] = _rms_norm(x_buf[...], g_ref[...]).astype(BF16)

    @pl.when((j == 1) & (i + 1 < pl.num_programs(0)))
    def _():
        x_copy(i + 1).start()

    def in_tiles(ranges):
        pred = None
        for lo, hi in ranges:
            p = (j >= lo) & (j < hi)
            pred = p if pred is None else (pred | p)
        return pred

    def run(epilogue, d):
        w = w_ref[...].astype(BF16)
        n = IN_CM // d
        for c in range(S // IN_CM):
            rows = pl.ds(c * IN_CM, IN_CM)
            acc = jnp.dot(h_ref[rows, :], w, preferred_element_type=F32)
            for s in range(n_slots):
                e = epilogue(acc[:, s * LANE:(s + 1) * LANE], rows)
                if d == 1:
                    o_ref[s, rows, :] = e.astype(BF16)
                    continue
                pair = 2 * ((c * n_slots + s) % (stage_ref.shape[0] // 2))
                stage, stage2 = stage_ref.at[pair], stage_ref.at[pair + 1]
                stage[...] = e
                if d == 16:
                    quarter = IN_CM // 4
                    for r4 in range(4):
                        stage2[pl.ds(r4 * quarter, quarter), :] = stage[pl.ds(r4, quarter, stride=4), :]
                    parts = {r4 + 4 * r2: stage2[pl.ds(r4 * quarter + r2, n, stride=4), :]
                             for r4 in range(4) for r2 in range(4)}
                else:
                    parts = {r: stage[pl.ds(r, n, stride=d), :] for r in range(d)}
                for r, part in parts.items():
                    o_ref[s, pl.ds(r * (S // d) + c * n, n), :] = part.astype(BF16)

    epilogues = {
        "rope": lambda a, rows: a * cos_ref[rows, :] + pltpu.roll(a, HEAD_DIM // 2, 1) * sin_ref[rows, :],
        "ident": lambda a, rows: a,
        "silu": lambda a, rows: a * jax.nn.sigmoid(a),
        "sigmoid": lambda a, rows: jax.nn.sigmoid(a),
    }
    for (kind, d), ranges in tile_kinds.items():
        pl.when(in_tiles(ranges))(functools.partial(run, epilogues[kind], d))


def _in_proj(x, g, w, cos_t, sin_t):
    B, S, D = x.shape
    width = w.shape[1]
    tn = IN_TN
    assert S % IN_CM == 0 and width % tn == 0 and tn % LANE == 0
    tile_kinds = {}
    col = 0
    for _, seg_w, kind, d in _segments(D):
        assert col % tn == 0 and seg_w % tn == 0 and (IN_CM // d) % BF16_SUBLANES == 0 and d in (1, 4, 16)
        tile_kinds.setdefault((kind, d), []).append((col // tn, (col + seg_w) // tn))
        col += seg_w
    assert col == width

    def table_spec():
        return pl.BlockSpec((S, LANE), lambda i, j: (0, 0), pipeline_mode=pl.Buffered(1))

    return pl.pallas_call(
        functools.partial(_in_proj_kernel, tile_kinds=tile_kinds),
        grid=(B, width // tn),
        in_specs=[
            pl.BlockSpec(memory_space=pltpu.HBM),
            pl.BlockSpec((1, D), lambda i, j: (0, 0)),
            pl.BlockSpec((D, tn), lambda i, j: (0, j)),
            table_spec(), table_spec(),
        ],
        out_specs=pl.BlockSpec((None, tn // LANE, S, LANE), lambda i, j: (i, j, 0, 0)),
        out_shape=jax.ShapeDtypeStruct((B, width // LANE, S, LANE), BF16),
        scratch_shapes=[pltpu.VMEM((S, D), F32), pltpu.VMEM((S, D), BF16),
                        pltpu.VMEM((IN_STAGE_BUFS, IN_CM, LANE), F32), pltpu.SemaphoreType.DMA(())],
        compiler_params=pltpu.CompilerParams(
            dimension_semantics=("arbitrary", "arbitrary"),
            vmem_limit_bytes=VMEM_LIMIT_BYTES),
        name="in_proj",
    )(x, g.reshape(1, D), w, cos_t, sin_t)


def _reduce_rows(op, a):
    rows, n = a.shape
    chain = 16
    a = a.reshape(rows // (chain * F32_SUBLANES), chain, F32_SUBLANES, n)
    return op(op(op(a, axis=1), axis=0), axis=0, keepdims=True)


def _moba_kernel(q_ref, k_ref, v_ref, z_ref, o_ref):
    S = q_ref.shape[0]
    blk = MOBA_BLOCK
    nb = S // blk
    scale = HEAD_DIM ** -0.5
    k = k_ref[...]
    q_t = q_ref[...].T
    v_t = v_ref[...].T
    v_t = jnp.concatenate([v_t, jnp.ones((BF16_SUBLANES, S), BF16)], axis=0)

    kmean = jnp.mean(k.astype(F32).reshape(nb, blk, HEAD_DIM), axis=1)
    k_hi = kmean.astype(BF16).astype(F32)
    k_hl = jnp.concatenate([k_hi, kmean - k_hi], axis=0).astype(BF16)
    gate2 = jnp.dot(k_hl, q_t, preferred_element_type=F32)
    gate = gate2[:nb] + gate2[nb:]

    key_row = lax.broadcasted_iota(jnp.int32, (blk, blk), 0)
    query_col = lax.broadcasted_iota(jnp.int32, (blk, blk), 1)
    blk_row = lax.broadcasted_iota(jnp.int32, (nb, blk), 0)

    key_blk = lax.broadcasted_iota(jnp.int32, (S, LANE), 0) // blk
    onehot = (key_blk == lax.broadcasted_iota(jnp.int32, (S, LANE), 1)).astype(BF16)
    k_aug = jnp.concatenate([k, onehot], axis=1)

    def cols(i):
        return slice(i * blk, (i + 1) * blk)

    def scores(i):
        kv_len = (i + 1) * blk
        if i == 0:
            s = jnp.dot(k[:kv_len], q_t[:, cols(i)], preferred_element_type=F32)
            return jnp.where(key_row <= query_col, s, NEG)
        g = gate[:, cols(i)]
        rank = jnp.zeros((nb, blk), jnp.int32)
        for jp in range(i):
            c = g[jp:jp + 1, :]
            beats = (c > g) | ((c == g) & (blk_row > jp))
            rank = rank + beats.astype(jnp.int32)
        bias = jnp.where((rank >= MOBA_TOPK) & (blk_row < i), NEG, 0.0)
        bias = jnp.concatenate([bias, jnp.zeros((LANE - nb, blk), F32)], axis=0).astype(BF16)
        q_aug = jnp.concatenate([q_t[:, cols(i)], bias], axis=0)
        s = jnp.dot(k_aug[:kv_len], q_aug, preferred_element_type=F32)
        own = jnp.where(key_row <= query_col, s[i * blk:], NEG)
        return jnp.concatenate([s[:i * blk], own], axis=0)

    s_all = [scores(i) for i in range(nb)]
    m = [_reduce_rows(jnp.max, s) for s in s_all]
    p = [jnp.exp2((s - mi) * (scale * _LOG2E)) for s, mi in zip(s_all, m)]
    pv_t = [jnp.dot(v_t[:, :(i + 1) * blk], p[i].astype(BF16), preferred_element_type=F32)
            for i in range(nb)]
    for i in range(nb):
        out = (pv_t[i][:HEAD_DIM] / pv_t[i][HEAD_DIM:HEAD_DIM + 1]).T * z_ref[cols(i), :].astype(F32)
        o_ref[cols(i), :] = out.astype(BF16)


def _moba(p_slots, offs):
    B, _, S, _ = p_slots.shape
    assert S % MOBA_BLOCK == 0 and S // MOBA_BLOCK <= LANE

    def slab(base):
        return pl.BlockSpec((None, None, S, LANE), lambda b, h: (b, base + h, 0, 0))

    return pl.pallas_call(
        _moba_kernel,
        grid=(B, MOBA_HEADS),
        in_specs=[slab(offs["qa"]), slab(offs["ka"]), slab(offs["va"]), slab(offs["za"])],
        out_specs=pl.BlockSpec((None, None, S, LANE), lambda b, h: (b, h, 0, 0)),
        out_shape=jax.ShapeDtypeStruct((B, MOBA_HEADS, S, LANE), BF16),
        compiler_params=pltpu.CompilerParams(
            dimension_semantics=("arbitrary", "arbitrary"),
            vmem_limit_bytes=VMEM_LIMIT_BYTES),
        name="moba",
    )(p_slots, p_slots, p_slots, p_slots)


def _dilated_kernel(q0, q1, q2, k0, k1, k2, v0, v1, v2, z_ref, o_ref, o_scr, l_scr):
    S = z_ref.shape[0]
    scale = HEAD_DIM ** -0.5
    q_refs, k_refs, v_refs = (q0, q1, q2), (k0, k1, k2), (v0, v1, v2)

    band = LANE
    n_blocks = S // band
    row = lax.broadcasted_iota(jnp.int32, (band, 2 * band), 0)
    col = lax.broadcasted_iota(jnp.int32, (band, 2 * band), 1)
    mask_rest = (col >= row) & (col <= row + band)
    mask_first = mask_rest & (col >= band)
    ones = jnp.ones((2 * band, LANE), BF16)

    for g, (window, d) in enumerate(DIL_CONFIGS):
        assert window // d == band
        nblk = S // d // band
        q = q_refs[g][...]
        k = k_refs[g][...]
        v = v_refs[g][...]

        def rows(b):
            return slice(b * band, (b + 1) * band)

        def window_rows(t, b):
            if b % nblk == 0:
                return jnp.concatenate([t[rows(b)], t[rows(b)]], axis=0)
            return t[(b - 1) * band:(b + 1) * band]

        s = jnp.concatenate(
            [jnp.where(mask_first if b % nblk == 0 else mask_rest,
                       lax.dot_general(q[rows(b)], window_rows(k, b), _CONTRACT_LAST,
                                       preferred_element_type=F32), NEG)
             for b in range(n_blocks)], axis=0)
        m = jnp.max(s, axis=1, keepdims=True)
        p = jnp.exp2((s - m) * (scale * _LOG2E))
        p = p.astype(BF16)
        for b in range(n_blocks):
            pv_l = jnp.dot(p[rows(b)], jnp.concatenate([window_rows(v, b), ones], axis=1),
                           preferred_element_type=F32)
            l = pv_l[:, HEAD_DIM:]
            r, n = divmod(b, nblk)
            dst = pl.ds(r + n * band * d, band, stride=d) if d > 1 else pl.ds(b * band, band)
            o_scr[g, dst, :] = pv_l[:, :HEAD_DIM] / l
            l_scr[g, dst, :] = m[rows(b)] * scale + jnp.log(l)

    lses = [l_scr[g] for g in range(len(DIL_CONFIGS))]
    m = functools.reduce(jnp.maximum, lses)
    es = [jnp.exp(l - m) for l in lses]
    den = functools.reduce(lambda a, b: a + b, es)
    num = functools.reduce(lambda a, b: a + b, [(e / den) * o_scr[g] for g, e in enumerate(es)])
    o_ref[...] = (num * z_ref[...].astype(F32)).astype(BF16)


def _dilated(p_slots, offs):
    B, _, S, _ = p_slots.shape
    G = len(DIL_CONFIGS)
    hg = DIL_HEADS_PER_GROUP
    for window, d in DIL_CONFIGS:
        band = window // d
        assert band == LANE and S % (d * band) == 0

    def slab(base):
        return pl.BlockSpec((None, None, S, LANE), lambda b, h: (b, base + h, 0, 0))

    in_specs = ([slab(offs[f"{name}{g}"]) for name in ("qb", "kb", "vb") for g in range(G)]
                + [slab(offs["zb"])])
    return pl.pallas_call(
        _dilated_kernel,
        grid=(B, hg),
        in_specs=in_specs,
        out_specs=pl.BlockSpec((None, None, S, LANE), lambda b, h: (b, h, 0, 0)),
        out_shape=jax.ShapeDtypeStruct((B, hg, S, LANE), BF16),
        scratch_shapes=[pltpu.VMEM((G, S, LANE), F32)] * 2,
        compiler_params=pltpu.CompilerParams(
            dimension_semantics=("arbitrary", "arbitrary"),
            vmem_limit_bytes=VMEM_LIMIT_BYTES),
        name="dilated",
    )(*([p_slots] * (3 * G + 1)))


def _mem_kv_kernel(mem_ref, g_ref, w_ref, o_ref, wb_ref):
    @pl.when(pl.program_id(0) == 0)
    def _():
        wb_ref[...] = w_ref[...].astype(BF16)

    h = _rms_norm(mem_ref[...], g_ref[...]).astype(BF16)
    o_ref[...] = jnp.dot(h, wb_ref[...], preferred_element_type=F32).astype(BF16)


def _mem_kv(mem, g, w):
    B, M, D = mem.shape
    width = w.shape[1]
    return pl.pallas_call(
        _mem_kv_kernel,
        grid=(B,),
        in_specs=[
            pl.BlockSpec((None, M, D), lambda b: (b, 0, 0)),
            pl.BlockSpec((1, D), lambda b: (0, 0)),
            pl.BlockSpec((D, width), lambda b: (0, 0), pipeline_mode=pl.Buffered(1)),
        ],
        out_specs=pl.BlockSpec((None, M, width), lambda b: (b, 0, 0)),
        out_shape=jax.ShapeDtypeStruct((B, M, width), BF16),
        scratch_shapes=[pltpu.VMEM((D, width), BF16)],
        compiler_params=pltpu.CompilerParams(
            dimension_semantics=("arbitrary",),
            vmem_limit_bytes=VMEM_LIMIT_BYTES),
        name="mem_kv",
    )(mem, g.reshape(1, D), w)


def _load_weights_bf16(pairs, stage_ref, sems):
    rows = stage_ref.shape[1]
    chunks = [(src, dst, r0) for src, dst in pairs for r0 in range(0, src.shape[0], rows)]

    def copy(n):
        src, _, r0 = chunks[n]
        return pltpu.make_async_copy(src.at[pl.ds(r0, rows)], stage_ref.at[n % 2], sems.at[n % 2])

    copy(0).start()
    for n, (_, dst, r0) in enumerate(chunks):
        if n + 1 < len(chunks):
            copy(n + 1).start()
        copy(n).wait()
        dst[pl.ds(r0, rows), :] = stage_ref[n % 2].astype(BF16)


def _out_proj_kernel(x_ref, ya_ref, yb_ref, qm_ref, zm_ref, ga0, ga1, gb0, gb1, gm0, gm1, mkv_ref,
                     wa_hbm, wb_hbm, wm_hbm, wo_hbm, gf_ref, o_ref,
                     wa_ref, wb_ref, wm_ref, wo_ref, stage_ref, w_sems, *, final_norm):
    @pl.when((pl.program_id(0) == 0) & (pl.program_id(1) == 0))
    def _():
        _load_weights_bf16(((wa_hbm, wa_ref), (wb_hbm, wb_ref), (wm_hbm, wm_ref), (wo_hbm, wo_ref)),
                           stage_ref, w_sems)

    def slots(ref):
        return jnp.concatenate([ref[s] for s in range(ref.shape[0])], axis=1)

    def gate(r0, r1):
        return jnp.concatenate([slots(r0), slots(r1)], axis=1).astype(F32)

    merged = (gate(ga0, ga1) * jnp.dot(slots(ya_ref), wa_ref[...], preferred_element_type=F32)
              + gate(gb0, gb1) * jnp.dot(slots(yb_ref), wb_ref[...], preferred_element_type=F32))

    w_m = MEM_HEADS * MEM_HEAD_DIM
    scale = MEM_HEAD_DIM ** -0.5
    qm = slots(qm_ref)
    zm = slots(zm_ref)
    heads = [slice(h * MEM_HEAD_DIM, (h + 1) * MEM_HEAD_DIM) for h in range(MEM_HEADS)]
    s = [lax.dot_general(qm[:, c], mkv_ref[:, c], _CONTRACT_LAST, preferred_element_type=F32) for c in heads]
    m = [jnp.max(si, axis=1, keepdims=True) for si in s]
    p = [jnp.exp2((si - mi) * (scale * _LOG2E)) for si, mi in zip(s, m)]
    l = [jnp.sum(pi, axis=1, keepdims=True) for pi in p]
    pv = [jnp.dot(pi.astype(BF16), mkv_ref[:, w_m + c.start:w_m + c.stop], preferred_element_type=F32)
          for pi, c in zip(p, heads)]
    ym = jnp.concatenate([((pvi / li) * zm[:, c].astype(F32)).astype(BF16)
                          for pvi, li, c in zip(pv, l, heads)], axis=1)
    merged = merged + gate(gm0, gm1) * jnp.dot(ym, wm_ref[...], preferred_element_type=F32)
    y = x_ref[...] + jnp.dot(merged.astype(BF16), wo_ref[...], preferred_element_type=F32)
    o_ref[...] = _rms_norm(y, gf_ref[...]) if final_norm else y


def _out_proj(x, ya, yb, p_slots, offs, mkv, wa, wb, wm, wo, g_final, final_norm):
    B, S, D = x.shape
    tm = OUT_TM
    gslots = D // LANE // 2
    assert S % tm == 0 and offs["gates"] % gslots == 0 and gslots == MOBA_HEADS
    assert offs["qm"] % gslots == 0 and offs["zm"] % gslots == 0

    def pblock(slot0):
        blk = slot0 // gslots
        return pl.BlockSpec((None, gslots, tm, LANE), lambda b, s: (b, blk, s, 0))

    weights = (wa, wb, wm, wo)
    assert all(w.shape[1] == D and w.shape[0] % OUT_W_ROWS == 0 for w in weights)
    gate_specs = [pblock(offs["gates"] + i * gslots) for i in range(2 * N_BRANCH)]
    return pl.pallas_call(
        functools.partial(_out_proj_kernel, final_norm=final_norm),
        grid=(B, S // tm),
        in_specs=[
            pl.BlockSpec((None, tm, D), lambda b, s: (b, s, 0)),
            pl.BlockSpec((None, MOBA_HEADS, tm, LANE), lambda b, s: (b, 0, s, 0)),
            pl.BlockSpec((None, DIL_HEADS_PER_GROUP, tm, LANE), lambda b, s: (b, 0, s, 0)),
            pblock(offs["qm"]), pblock(offs["zm"]), *gate_specs,
            pl.BlockSpec((None,) + mkv.shape[1:], lambda b, s: (b, 0, 0)),
            *([pl.BlockSpec(memory_space=pltpu.HBM)] * len(weights)),
            pl.BlockSpec((1, D), lambda b, s: (0, 0)),
        ],
        out_specs=pl.BlockSpec((None, tm, D), lambda b, s: (b, s, 0)),
        out_shape=jax.ShapeDtypeStruct((B, S, D), F32),
        scratch_shapes=[pltpu.VMEM(w.shape, BF16) for w in weights]
        + [pltpu.VMEM((2, OUT_W_ROWS, D), F32), pltpu.SemaphoreType.DMA((2,))],
        compiler_params=pltpu.CompilerParams(
            dimension_semantics=("arbitrary", "arbitrary"),
            vmem_limit_bytes=VMEM_LIMIT_BYTES),
        name="out_proj",
    )(x, ya, yb, *([p_slots] * (2 + 2 * N_BRANCH)), mkv, wa, wb, wm, wo, g_final.reshape(1, D))


def _rope_tables(S):
    half = HEAD_DIM // 2
    inv = ROPE_THETA ** (-jnp.arange(half, dtype=F32) / half)
    ang = jnp.arange(S, dtype=F32)[:, None] * inv[None, :]
    cos, sin = jnp.cos(ang), jnp.sin(ang)
    return jnp.concatenate([cos, cos], axis=1), jnp.concatenate([-sin, sin], axis=1)


def kernel(x, mem, norm_in_g, norm_mem_g, w_in, w_mem_kv, w_proj_a, w_proj_b, w_proj_m, w_out, norm_final_g):
    B, S, D = x.shape
    depth = w_in.shape[0]
    offs, n_slots = _slot_offsets(D)
    assert w_in.shape[2] == n_slots * LANE
    cos_t, sin_t = _rope_tables(S)
    for layer in range(depth):
        p_slots = _in_proj(x, norm_in_g[layer], w_in[layer], cos_t, sin_t)
        ya = _moba(p_slots, offs)
        yb = _dilated(p_slots, offs)
        mkv = _mem_kv(mem, norm_mem_g, w_mem_kv[layer])
        x = _out_proj(x, ya, yb, p_slots, offs, mkv,
                      w_proj_a[layer], w_proj_b[layer], w_proj_m[layer], w_out[layer],
                      norm_final_g, final_norm=(layer == depth - 1))
    return x
```

```python
import functools

import jax
import jax.numpy as jnp
from jax import lax
from jax.experimental import pallas as pl
from jax.experimental.pallas import tpu as pltpu

F32 = jnp.float32
BF16 = jnp.bfloat16

HEAD_DIM = 128
MOBA_HEADS = 8
MOBA_BLOCK = 256
MOBA_TOPK = 3
DIL_CONFIGS = ((128, 1), (512, 4), (2048, 16))
DIL_HEADS_PER_GROUP = 4
MEM_HEADS = 4
MEM_HEAD_DIM = 256
ROPE_THETA = 10000.0
RMS_EPS = 1e-6
NEG = -1e30
N_BRANCH = 3

LANE = 128
F32_SUBLANES = 8
BF16_SUBLANES = 16
V7X_VMEM_BYTES = 64 * 1024 * 1024
VMEM_LIMIT_BYTES = V7X_VMEM_BYTES - 8 * 1024 * 1024

IN_TN = 512
IN_CM = 256
IN_STAGE_BUFS = 8
MOBA_HEADS_PER_STEP = 2
OUT_TM = 256
OUT_W_ROWS = 256

_CONTRACT_LAST = (((1,), (1,)), ((), ()))
_LOG2E = 1.4426950408889634


def _segments(d_model):
    w_a = MOBA_HEADS * HEAD_DIM
    w_g = DIL_HEADS_PER_GROUP * HEAD_DIM
    w_m = MEM_HEADS * MEM_HEAD_DIM

    def groups(name, kind):
        return tuple((f"{name}{g}", w_g, kind, d) for g, (_, d) in enumerate(DIL_CONFIGS))

    return (("qa", w_a, "rope", 1), ("ka", w_a, "rope", 1), ("va", w_a, "ident", 1), ("za", w_a, "silu", 1),
            *groups("qb", "rope"), *groups("kb", "rope"), *groups("vb", "ident"), ("zb", w_g, "silu", 1),
            ("qm", w_m, "ident", 1), ("zm", w_m, "silu", 1), ("gates", N_BRANCH * d_model, "sigmoid", 1))


def _slot_offsets(d_model):
    offs, col = {}, 0
    for name, width, _, _ in _segments(d_model):
        offs[name] = col // LANE
        col += width
    return offs, col // LANE


def _rms_norm(xf, g):
    ms = jnp.mean(xf * xf, axis=-1, keepdims=True)
    return (xf * lax.rsqrt(ms + RMS_EPS)) * g


def _in_proj_kernel(x_hbm, g_ref, w_ref, cos_ref, sin_ref, o_ref, x_buf, h_ref, stage_ref, x_sem, *,
                    tile_kinds):
    i = pl.program_id(0)
    j = pl.program_id(1)
    n_slots = o_ref.shape[0]
    S = h_ref.shape[0]

    def x_copy(b):
        return pltpu.make_async_copy(x_hbm.at[b], x_buf, x_sem)

    @pl.when((i == 0) & (j == 0))
    def _():
        x_copy(0).start()

    @pl.when(j == 0)
    def _():
        x_copy(i).wait()
        h_ref[...] = _rms_norm(x_buf[...], g_ref[...]).astype(BF16)

    @pl.when((j == 1) & (i + 1 < pl.num_programs(0)))
    def _():
        x_copy(i + 1).start()

    def in_tiles(ranges):
        pred = None
        for lo, hi in ranges:
            p = (j >= lo) & (j < hi)
            pred = p if pred is None else (pred | p)
        return pred

    def run(epilogue, d):
        w = w_ref[...].astype(BF16)
        n = IN_CM // d
        for c in range(S // IN_CM):
            rows = pl.ds(c * IN_CM, IN_CM)
            acc = jnp.dot(h_ref[rows, :], w, preferred_element_type=F32)
            for s in range(n_slots):
                e = epilogue(acc[:, s * LANE:(s + 1) * LANE], rows)
                if d == 1:
                    o_ref[s, rows, :] = e.astype(BF16)
                    continue
                pair = 2 * ((c * n_slots + s) % (stage_ref.shape[0] // 2))
                stage, stage2 = stage_ref.at[pair], stage_ref.at[pair + 1]
                stage[...] = e
                if d == 16:
                    quarter = IN_CM // 4
                    for r4 in range(4):
                        stage2[pl.ds(r4 * quarter, quarter), :] = stage[pl.ds(r4, quarter, stride=4), :]
                    parts = {r4 + 4 * r2: stage2[pl.ds(r4 * quarter + r2, n, stride=4), :]
                             for r4 in range(4) for r2 in range(4)}
                else:
                    parts = {r: stage[pl.ds(r, n, stride=d), :] for r in range(d)}
                for r, part in parts.items():
                    o_ref[s, pl.ds(r * (S // d) + c * n, n), :] = part.astype(BF16)

    epilogues = {
        "rope": lambda a, rows: a * cos_ref[rows, :] + pltpu.roll(a, HEAD_DIM // 2, 1) * sin_ref[rows, :],
        "ident": lambda a, rows: a,
        "silu": lambda a, rows: a * jax.nn.sigmoid(a),
        "sigmoid": lambda a, rows: jax.nn.sigmoid(a),
    }
    for (kind, d), ranges in tile_kinds.items():
        pl.when(in_tiles(ranges))(functools.partial(run, epilogues[kind], d))


def _in_proj(x, g, w, cos_t, sin_t):
    B, S, D = x.shape
    width = w.shape[1]
    tn = IN_TN
    assert S % IN_CM == 0 and width % tn == 0 and tn % LANE == 0
    tile_kinds = {}
    col = 0
    for _, seg_w, kind, d in _segments(D):
        assert col % tn == 0 and seg_w % tn == 0 and (IN_CM // d) % BF16_SUBLANES == 0 and d in (1, 4, 16)
        tile_kinds.setdefault((kind, d), []).append((col // tn, (col + seg_w) // tn))
        col += seg_w
    assert col == width

    def table_spec():
        return pl.BlockSpec((S, LANE), lambda i, j: (0, 0), pipeline_mode=pl.Buffered(1))

    return pl.pallas_call(
        functools.partial(_in_proj_kernel, tile_kinds=tile_kinds),
        grid=(B, width // tn),
        in_specs=[
            pl.BlockSpec(memory_space=pltpu.HBM),
            pl.BlockSpec((1, D), lambda i, j: (0, 0)),
            pl.BlockSpec((D, tn), lambda i, j: (0, j)),
            table_spec(), table_spec(),
        ],
        out_specs=pl.BlockSpec((None, tn // LANE, S, LANE), lambda i, j: (i, j, 0, 0)),
        out_shape=jax.ShapeDtypeStruct((B, width // LANE, S, LANE), BF16),
        scratch_shapes=[pltpu.VMEM((S, D), F32), pltpu.VMEM((S, D), BF16),
                        pltpu.VMEM((IN_STAGE_BUFS, IN_CM, LANE), F32), pltpu.SemaphoreType.DMA(())],
        compiler_params=pltpu.CompilerParams(
            dimension_semantics=("arbitrary", "arbitrary"),
            vmem_limit_bytes=VMEM_LIMIT_BYTES),
        name="in_proj",
    )(x, g.reshape(1, D), w, cos_t, sin_t)


def _reduce_rows(op, a):
    rows, n = a.shape
    chain = 16
    a = a.reshape(rows // (chain * F32_SUBLANES), chain, F32_SUBLANES, n)
    return op(op(op(a, axis=1), axis=0), axis=0, keepdims=True)


def _moba_head(q_ref, k_ref, v_ref, z_ref, o_ref):
    S = q_ref.shape[0]
    blk = MOBA_BLOCK
    nb = S // blk
    scale = HEAD_DIM ** -0.5
    k = k_ref[...]
    q_t = q_ref[...].T
    v_t = v_ref[...].T
    v_t = jnp.concatenate([v_t, jnp.ones((BF16_SUBLANES, S), BF16)], axis=0)

    kmean = jnp.mean(k.astype(F32).reshape(nb, blk, HEAD_DIM), axis=1)
    k_hi = kmean.astype(BF16).astype(F32)
    k_hl = jnp.concatenate([k_hi, kmean - k_hi], axis=0).astype(BF16)
    gate2 = jnp.dot(k_hl, q_t, preferred_element_type=F32)
    gate = gate2[:nb] + gate2[nb:]

    key_row = lax.broadcasted_iota(jnp.int32, (blk, blk), 0)
    query_col = lax.broadcasted_iota(jnp.int32, (blk, blk), 1)
    blk_row = lax.broadcasted_iota(jnp.int32, (nb, blk), 0)

    key_blk = lax.broadcasted_iota(jnp.int32, (S, LANE), 0) // blk
    onehot = (key_blk == lax.broadcasted_iota(jnp.int32, (S, LANE), 1)).astype(BF16)
    k_aug = jnp.concatenate([k, onehot], axis=1)

    def cols(i):
        return slice(i * blk, (i + 1) * blk)

    def scores(i):
        kv_len = (i + 1) * blk
        if i == 0:
            s = jnp.dot(k[:kv_len], q_t[:, cols(i)], preferred_element_type=F32)
            return jnp.where(key_row <= query_col, s, NEG)
        g = gate[:, cols(i)]
        rank = jnp.zeros((nb, blk), jnp.int32)
        for jp in range(i):
            c = g[jp:jp + 1, :]
            beats = (c > g) | ((c == g) & (blk_row > jp))
            rank = rank + beats.astype(jnp.int32)
        bias = jnp.where((rank >= MOBA_TOPK) & (blk_row < i), NEG, 0.0)
        bias = jnp.concatenate([bias, jnp.zeros((LANE - nb, blk), F32)], axis=0).astype(BF16)
        q_aug = jnp.concatenate([q_t[:, cols(i)], bias], axis=0)
        s = jnp.dot(k_aug[:kv_len], q_aug, preferred_element_type=F32)
        own = jnp.where(key_row <= query_col, s[i * blk:], NEG)
        return jnp.concatenate([s[:i * blk], own], axis=0)

    def finish(i, pv_t):
        out = (pv_t[:HEAD_DIM] / pv_t[HEAD_DIM:HEAD_DIM + 1]).T * z_ref[cols(i), :].astype(F32)
        o_ref[cols(i), :] = out.astype(BF16)

    def pv(i, p):
        return jnp.dot(v_t[:, :(i + 1) * blk], p.astype(BF16), preferred_element_type=F32)

    return nb, scores, pv, finish


def _moba_kernel(q_ref, k_ref, v_ref, z_ref, o_ref):
    scale = HEAD_DIM ** -0.5
    heads = [_moba_head(q_ref.at[h], k_ref.at[h], v_ref.at[h], z_ref.at[h], o_ref.at[h])
             for h in range(q_ref.shape[0])]
    work = [(scores, pv, finish, i) for nb, scores, pv, finish in heads for i in range(nb)]
    s_all = [scores(i) for scores, _, _, i in work]
    m = [_reduce_rows(jnp.max, s) for s in s_all]
    p = [jnp.exp2((s - mi) * (scale * _LOG2E)) for s, mi in zip(s_all, m)]
    pv_t = [pv(i, pi) for (_, pv, _, i), pi in zip(work, p)]
    for (_, _, finish, i), t in zip(work, pv_t):
        finish(i, t)


def _moba(p_slots, offs):
    B, _, S, _ = p_slots.shape
    assert S % MOBA_BLOCK == 0 and S // MOBA_BLOCK <= LANE

    hps = MOBA_HEADS_PER_STEP
    assert MOBA_HEADS % hps == 0 and all(offs[n] % hps == 0 for n in ("qa", "ka", "va", "za"))

    def slab(base):
        return pl.BlockSpec((None, hps, S, LANE), lambda b, h: (b, base // hps + h, 0, 0))

    return pl.pallas_call(
        _moba_kernel,
        grid=(B, MOBA_HEADS // hps),
        in_specs=[slab(offs["qa"]), slab(offs["ka"]), slab(offs["va"]), slab(offs["za"])],
        out_specs=pl.BlockSpec((None, hps, S, LANE), lambda b, h: (b, h, 0, 0)),
        out_shape=jax.ShapeDtypeStruct((B, MOBA_HEADS, S, LANE), BF16),
        compiler_params=pltpu.CompilerParams(
            dimension_semantics=("arbitrary", "arbitrary"),
            vmem_limit_bytes=VMEM_LIMIT_BYTES),
        name="moba",
    )(p_slots, p_slots, p_slots, p_slots)


def _dilated_kernel(q0, q1, q2, k0, k1, k2, v0, v1, v2, z_ref, o_ref, o_scr, l_scr):
    S = z_ref.shape[0]
    scale = HEAD_DIM ** -0.5
    q_refs, k_refs, v_refs = (q0, q1, q2), (k0, k1, k2), (v0, v1, v2)

    band = LANE
    n_blocks = S // band
    row = lax.broadcasted_iota(jnp.int32, (band, 2 * band), 0)
    col = lax.broadcasted_iota(jnp.int32, (band, 2 * band), 1)
    mask_rest = (col >= row) & (col <= row + band)
    mask_first = mask_rest & (col >= band)
    ones = jnp.ones((2 * band, LANE), BF16)

    for g, (window, d) in enumerate(DIL_CONFIGS):
        assert window // d == band
        nblk = S // d // band
        q = q_refs[g][...]
        k = k_refs[g][...]
        v = v_refs[g][...]

        def rows(b):
            return slice(b * band, (b + 1) * band)

        def window_rows(t, b):
            if b % nblk == 0:
                return jnp.concatenate([t[rows(b)], t[rows(b)]], axis=0)
            return t[(b - 1) * band:(b + 1) * band]

        s = jnp.concatenate(
            [jnp.where(mask_first if b % nblk == 0 else mask_rest,
                       lax.dot_general(q[rows(b)], window_rows(k, b), _CONTRACT_LAST,
                                       preferred_element_type=F32), NEG)
             for b in range(n_blocks)], axis=0)
        m = jnp.max(s, axis=1, keepdims=True)
        p = jnp.exp2((s - m) * (scale * _LOG2E))
        p = p.astype(BF16)
        for b in range(n_blocks):
            pv_l = jnp.dot(p[rows(b)], jnp.concatenate([window_rows(v, b), ones], axis=1),
                           preferred_element_type=F32)
            l = pv_l[:, HEAD_DIM:]
            r, n = divmod(b, nblk)
            dst = pl.ds(r + n * band * d, band, stride=d) if d > 1 else pl.ds(b * band, band)
            o_scr[g, dst, :] = pv_l[:, :HEAD_DIM] / l
            l_scr[g, dst, :] = m[rows(b)] * scale + jnp.log(l)

    lses = [l_scr[g] for g in range(len(DIL_CONFIGS))]
    m = functools.reduce(jnp.maximum, lses)
    es = [jnp.exp(l - m) for l in lses]
    den = functools.reduce(lambda a, b: a + b, es)
    num = functools.reduce(lambda a, b: a + b, [(e / den) * o_scr[g] for g, e in enumerate(es)])
    o_ref[...] = (num * z_ref[...].astype(F32)).astype(BF16)


def _dilated(p_slots, offs):
    B, _, S, _ = p_slots.shape
    G = len(DIL_CONFIGS)
    hg = DIL_HEADS_PER_GROUP
    for window, d in DIL_CONFIGS:
        band = window // d
        assert band == LANE and S % (d * band) == 0

    def slab(base):
        return pl.BlockSpec((None, None, S, LANE), lambda b, h: (b, base + h, 0, 0))

    in_specs = ([slab(offs[f"{name}{g}"]) for name in ("qb", "kb", "vb") for g in range(G)]
                + [slab(offs["zb"])])
    return pl.pallas_call(
        _dilated_kernel,
        grid=(B, hg),
        in_specs=in_specs,
        out_specs=pl.BlockSpec((None, None, S, LANE), lambda b, h: (b, h, 0, 0)),
        out_shape=jax.ShapeDtypeStruct((B, hg, S, LANE), BF16),
        scratch_shapes=[pltpu.VMEM((G, S, LANE), F32)] * 2,
        compiler_params=pltpu.CompilerParams(
            dimension_semantics=("arbitrary", "arbitrary"),
            vmem_limit_bytes=VMEM_LIMIT_BYTES),
        name="dilated",
    )(*([p_slots] * (3 * G + 1)))


def _mem_kv_kernel(mem_ref, g_ref, w_ref, o_ref, wb_ref):
    @pl.when(pl.program_id(0) == 0)
    def _():
        wb_ref[...] = w_ref[...].astype(BF16)

    h = _rms_norm(mem_ref[...], g_ref[...]).astype(BF16)
    o_ref[...] = jnp.dot(h, wb_ref[...], preferred_element_type=F32).astype(BF16)


def _mem_kv(mem, g, w):
    B, M, D = mem.shape
    width = w.shape[1]
    return pl.pallas_call(
        _mem_kv_kernel,
        grid=(B,),
        in_specs=[
            pl.BlockSpec((None, M, D), lambda b: (b, 0, 0)),
            pl.BlockSpec((1, D), lambda b: (0, 0)),
            pl.BlockSpec((D, width), lambda b: (0, 0), pipeline_mode=pl.Buffered(1)),
        ],
        out_specs=pl.BlockSpec((None, M, width), lambda b: (b, 0, 0)),
        out_shape=jax.ShapeDtypeStruct((B, M, width), BF16),
        scratch_shapes=[pltpu.VMEM((D, width), BF16)],
        compiler_params=pltpu.CompilerParams(
            dimension_semantics=("arbitrary",),
            vmem_limit_bytes=VMEM_LIMIT_BYTES),
        name="mem_kv",
    )(mem, g.reshape(1, D), w)


def _load_weights_bf16(pairs, stage_ref, sems):
    rows = stage_ref.shape[1]
    chunks = [(src, dst, r0) for src, dst in pairs for r0 in range(0, src.shape[0], rows)]

    def copy(n):
        src, _, r0 = chunks[n]
        return pltpu.make_async_copy(src.at[pl.ds(r0, rows)], stage_ref.at[n % 2], sems.at[n % 2])

    copy(0).start()
    for n, (_, dst, r0) in enumerate(chunks):
        if n + 1 < len(chunks):
            copy(n + 1).start()
        copy(n).wait()
        dst[pl.ds(r0, rows), :] = stage_ref[n % 2].astype(BF16)


def _out_proj_kernel(x_ref, ya_ref, yb_ref, qm_ref, zm_ref, ga0, ga1, gb0, gb1, gm0, gm1, mkv_ref,
                     wa_hbm, wb_hbm, wm_hbm, wo_hbm, gf_ref, o_ref,
                     wa_ref, wb_ref, wm_ref, wo_ref, stage_ref, w_sems, *, final_norm):
    @pl.when((pl.program_id(0) == 0) & (pl.program_id(1) == 0))
    def _():
        _load_weights_bf16(((wa_hbm, wa_ref), (wb_hbm, wb_ref), (wm_hbm, wm_ref), (wo_hbm, wo_ref)),
                           stage_ref, w_sems)

    def slots(ref):
        return jnp.concatenate([ref[s] for s in range(ref.shape[0])], axis=1)

    def gate(r0, r1):
        return jnp.concatenate([slots(r0), slots(r1)], axis=1).astype(F32)

    merged = (gate(ga0, ga1) * jnp.dot(slots(ya_ref), wa_ref[...], preferred_element_type=F32)
              + gate(gb0, gb1) * jnp.dot(slots(yb_ref), wb_ref[...], preferred_element_type=F32))

    w_m = MEM_HEADS * MEM_HEAD_DIM
    scale = MEM_HEAD_DIM ** -0.5
    qm = slots(qm_ref)
    zm = slots(zm_ref)
    heads = [slice(h * MEM_HEAD_DIM, (h + 1) * MEM_HEAD_DIM) for h in range(MEM_HEADS)]
    s = [lax.dot_general(qm[:, c], mkv_ref[:, c], _CONTRACT_LAST, preferred_element_type=F32) for c in heads]
    m = [jnp.max(si, axis=1, keepdims=True) for si in s]
    p = [jnp.exp2((si - mi) * (scale * _LOG2E)) for si, mi in zip(s, m)]
    l = [jnp.sum(pi, axis=1, keepdims=True) for pi in p]
    pv = [jnp.dot(pi.astype(BF16), mkv_ref[:, w_m + c.start:w_m + c.stop], preferred_element_type=F32)
          for pi, c in zip(p, heads)]
    ym = jnp.concatenate([((pvi / li) * zm[:, c].astype(F32)).astype(BF16)
                          for pvi, li, c in zip(pv, l, heads)], axis=1)
    merged = merged + gate(gm0, gm1) * jnp.dot(ym, wm_ref[...], preferred_element_type=F32)
    y = x_ref[...] + jnp.dot(merged.astype(BF16), wo_ref[...], preferred_element_type=F32)
    o_ref[...] = _rms_norm(y, gf_ref[...]) if final_norm else y


def _out_proj(x, ya, yb, p_slots, offs, mkv, wa, wb, wm, wo, g_final, final_norm):
    B, S, D = x.shape
    tm = OUT_TM
    gslots = D // LANE // 2
    assert S % tm == 0 and offs["gates"] % gslots == 0 and gslots == MOBA_HEADS
    assert offs["qm"] % gslots == 0 and offs["zm"] % gslots == 0

    def pblock(slot0):
        blk = slot0 // gslots
        return pl.BlockSpec((None, gslots, tm, LANE), lambda b, s: (b, blk, s, 0))

    weights = (wa, wb, wm, wo)
    assert all(w.shape[1] == D and w.shape[0] % OUT_W_ROWS == 0 for w in weights)
    gate_specs = [pblock(offs["gates"] + i * gslots) for i in range(2 * N_BRANCH)]
    return pl.pallas_call(
        functools.partial(_out_proj_kernel, final_norm=final_norm),
        grid=(B, S // tm),
        in_specs=[
            pl.BlockSpec((None, tm, D), lambda b, s: (b, s, 0)),
            pl.BlockSpec((None, MOBA_HEADS, tm, LANE), lambda b, s: (b, 0, s, 0)),
            pl.BlockSpec((None, DIL_HEADS_PER_GROUP, tm, LANE), lambda b, s: (b, 0, s, 0)),
            pblock(offs["qm"]), pblock(offs["zm"]), *gate_specs,
            pl.BlockSpec((None,) + mkv.shape[1:], lambda b, s: (b, 0, 0)),
            *([pl.BlockSpec(memory_space=pltpu.HBM)] * len(weights)),
            pl.BlockSpec((1, D), lambda b, s: (0, 0)),
        ],
        out_specs=pl.BlockSpec((None, tm, D), lambda b, s: (b, s, 0)),
        out_shape=jax.ShapeDtypeStruct((B, S, D), F32),
        scratch_shapes=[pltpu.VMEM(w.shape, BF16) for w in weights]
        + [pltpu.VMEM((2, OUT_W_ROWS, D), F32), pltpu.SemaphoreType.DMA((2,))],
        compiler_params=pltpu.CompilerParams(
            dimension_semantics=("arbitrary", "arbitrary"),
            vmem_limit_bytes=VMEM_LIMIT_BYTES),
        name="out_proj",
    )(x, ya, yb, *([p_slots] * (2 + 2 * N_BRANCH)), mkv, wa, wb, wm, wo, g_final.reshape(1, D))


def _rope_tables(S):
    half = HEAD_DIM // 2
    inv = ROPE_THETA ** (-jnp.arange(half, dtype=F32) / half)
    ang = jnp.arange(S, dtype=F32)[:, None] * inv[None, :]
    cos, sin = jnp.cos(ang), jnp.sin(ang)
    return jnp.concatenate([cos, cos], axis=1), jnp.concatenate([-sin, sin], axis=1)


def kernel(x, mem, norm_in_g, norm_mem_g, w_in, w_mem_kv, w_proj_a, w_proj_b, w_proj_m, w_out, norm_final_g):
    B, S, D = x.shape
    depth = w_in.shape[0]
    offs, n_slots = _slot_offsets(D)
    assert w_in.shape[2] == n_slots * LANE
    cos_t, sin_t = _rope_tables(S)
    for layer in range(depth):
        p_slots = _in_proj(x, norm_in_g[layer], w_in[layer], cos_t, sin_t)
        ya = _moba(p_slots, offs)
        yb = _dilated(p_slots, offs)
        mkv = _mem_kv(mem, norm_mem_g, w_mem_kv[layer])
        x = _out_proj(x, ya, yb, p_slots, offs, mkv,
                      w_proj_a[layer], w_proj_b[layer], w_proj_m[layer], w_out[layer],
                      norm_final_g, final_norm=(layer == depth - 1))
    return x
```

```python
import functools

import jax
import jax.numpy as jnp
from jax import lax
from jax.experimental import pallas as pl
from jax.experimental.pallas import tpu as pltpu

F32 = jnp.float32
BF16 = jnp.bfloat16

HEAD_DIM = 128
MOBA_HEADS = 8
MOBA_BLOCK = 256
MOBA_TOPK = 3
DIL_CONFIGS = ((128, 1), (512, 4), (2048, 16))
DIL_HEADS_PER_GROUP = 4
MEM_HEADS = 4
MEM_HEAD_DIM = 256
ROPE_THETA = 10000.0
RMS_EPS = 1e-6
NEG = -1e30
N_BRANCH = 3

LANE = 128
F32_SUBLANES = 8
BF16_SUBLANES = 16
V7X_VMEM_BYTES = 64 * 1024 * 1024
VMEM_LIMIT_BYTES = V7X_VMEM_BYTES - 8 * 1024 * 1024

IN_TN = 512
IN_CM = 256
IN_STAGE_BUFS = 8
MOBA_HEADS_PER_STEP = 4
OUT_TM = 256
OUT_W_ROWS = 256

_CONTRACT_LAST = (((1,), (1,)), ((), ()))
_LOG2E = 1.4426950408889634


def _segments(d_model):
    w_a = MOBA_HEADS * HEAD_DIM
    w_g = DIL_HEADS_PER_GROUP * HEAD_DIM
    w_m = MEM_HEADS * MEM_HEAD_DIM

    def groups(name, kind):
        return tuple((f"{name}{g}", w_g, kind, d) for g, (_, d) in enumerate(DIL_CONFIGS))

    return (("qa", w_a, "rope", 1), ("ka", w_a, "rope", 1), ("va", w_a, "ident", 1), ("za", w_a, "silu", 1),
            *groups("qb", "rope"), *groups("kb", "rope"), *groups("vb", "ident"), ("zb", w_g, "silu", 1),
            ("qm", w_m, "ident", 1), ("zm", w_m, "silu", 1), ("gates", N_BRANCH * d_model, "sigmoid", 1))


def _slot_offsets(d_model):
    offs, col = {}, 0
    for name, width, _, _ in _segments(d_model):
        offs[name] = col // LANE
        col += width
    return offs, col // LANE


def _rms_norm(xf, g):
    ms = jnp.mean(xf * xf, axis=-1, keepdims=True)
    return (xf * lax.rsqrt(ms + RMS_EPS)) * g


def _in_proj_kernel(x_hbm, g_ref, w_ref, cos_ref, sin_ref, o_ref, x_buf, h_ref, stage_ref, x_sem, *,
                    tile_kinds):
    i = pl.program_id(0)
    j = pl.program_id(1)
    n_slots = o_ref.shape[0]
    S = h_ref.shape[0]

    def x_copy(b):
        return pltpu.make_async_copy(x_hbm.at[b], x_buf, x_sem)

    @pl.when((i == 0) & (j == 0))
    def _():
        x_copy(0).start()

    @pl.when(j == 0)
    def _():
        x_copy(i).wait()
        h_ref[...] = _rms_norm(x_buf[...], g_ref[...]).astype(BF16)

    @pl.when((j == 1) & (i + 1 < pl.num_programs(0)))
    def _():
        x_copy(i + 1).start()

    def in_tiles(ranges):
        pred = None
        for lo, hi in ranges:
            p = (j >= lo) & (j < hi)
            pred = p if pred is None else (pred | p)
        return pred

    def run(epilogue, d):
        w = w_ref[...].astype(BF16)
        n = IN_CM // d
        for c in range(S // IN_CM):
            rows = pl.ds(c * IN_CM, IN_CM)
            acc = jnp.dot(h_ref[rows, :], w, preferred_element_type=F32)
            for s in range(n_slots):
                e = epilogue(acc[:, s * LANE:(s + 1) * LANE], rows)
                if d == 1:
                    o_ref[s, rows, :] = e.astype(BF16)
                    continue
                pair = 2 * ((c * n_slots + s) % (stage_ref.shape[0] // 2))
                stage, stage2 = stage_ref.at[pair], stage_ref.at[pair + 1]
                stage[...] = e
                if d == 16:
                    quarter = IN_CM // 4
                    for r4 in range(4):
                        stage2[pl.ds(r4 * quarter, quarter), :] = stage[pl.ds(r4, quarter, stride=4), :]
                    parts = {r4 + 4 * r2: stage2[pl.ds(r4 * quarter + r2, n, stride=4), :]
                             for r4 in range(4) for r2 in range(4)}
                else:
                    parts = {r: stage[pl.ds(r, n, stride=d), :] for r in range(d)}
                for r, part in parts.items():
                    o_ref[s, pl.ds(r * (S // d) + c * n, n), :] = part.astype(BF16)

    epilogues = {
        "rope": lambda a, rows: a * cos_ref[rows, :] + pltpu.roll(a, HEAD_DIM // 2, 1) * sin_ref[rows, :],
        "ident": lambda a, rows: a,
        "silu": lambda a, rows: a * jax.nn.sigmoid(a),
        "sigmoid": lambda a, rows: jax.nn.sigmoid(a),
    }
    for (kind, d), ranges in tile_kinds.items():
        pl.when(in_tiles(ranges))(functools.partial(run, epilogues[kind], d))


def _in_proj(x, g, w, cos_t, sin_t):
    B, S, D = x.shape
    width = w.shape[1]
    tn = IN_TN
    assert S % IN_CM == 0 and width % tn == 0 and tn % LANE == 0
    tile_kinds = {}
    col = 0
    for _, seg_w, kind, d in _segments(D):
        assert col % tn == 0 and seg_w % tn == 0 and (IN_CM // d) % BF16_SUBLANES == 0 and d in (1, 4, 16)
        tile_kinds.setdefault((kind, d), []).append((col // tn, (col + seg_w) // tn))
        col += seg_w
    assert col == width

    def table_spec():
        return pl.BlockSpec((S, LANE), lambda i, j: (0, 0), pipeline_mode=pl.Buffered(1))

    return pl.pallas_call(
        functools.partial(_in_proj_kernel, tile_kinds=tile_kinds),
        grid=(B, width // tn),
        in_specs=[
            pl.BlockSpec(memory_space=pltpu.HBM),
            pl.BlockSpec((1, D), lambda i, j: (0, 0)),
            pl.BlockSpec((D, tn), lambda i, j: (0, j)),
            table_spec(), table_spec(),
        ],
        out_specs=pl.BlockSpec((None, tn // LANE, S, LANE), lambda i, j: (i, j, 0, 0)),
        out_shape=jax.ShapeDtypeStruct((B, width // LANE, S, LANE), BF16),
        scratch_shapes=[pltpu.VMEM((S, D), F32), pltpu.VMEM((S, D), BF16),
                        pltpu.VMEM((IN_STAGE_BUFS, IN_CM, LANE), F32), pltpu.SemaphoreType.DMA(())],
        compiler_params=pltpu.CompilerParams(
            dimension_semantics=("arbitrary", "arbitrary"),
            vmem_limit_bytes=VMEM_LIMIT_BYTES),
        name="in_proj",
    )(x, g.reshape(1, D), w, cos_t, sin_t)


def _reduce_rows(op, a):
    rows, n = a.shape
    chain = 16
    a = a.reshape(rows // (chain * F32_SUBLANES), chain, F32_SUBLANES, n)
    return op(op(op(a, axis=1), axis=0), axis=0, keepdims=True)


def _moba_head(q_ref, k_ref, v_ref, z_ref, o_ref):
    S = q_ref.shape[0]
    blk = MOBA_BLOCK
    nb = S // blk
    scale = HEAD_DIM ** -0.5
    k = k_ref[...]
    q_t = q_ref[...].T
    v_t = v_ref[...].T
    v_t = jnp.concatenate([v_t, jnp.ones((BF16_SUBLANES, S), BF16)], axis=0)

    kmean = jnp.mean(k.astype(F32).reshape(nb, blk, HEAD_DIM), axis=1)
    k_hi = kmean.astype(BF16).astype(F32)
    k_hl = jnp.concatenate([k_hi, kmean - k_hi], axis=0).astype(BF16)
    gate2 = jnp.dot(k_hl, q_t, preferred_element_type=F32)
    gate = gate2[:nb] + gate2[nb:]

    key_row = lax.broadcasted_iota(jnp.int32, (blk, blk), 0)
    query_col = lax.broadcasted_iota(jnp.int32, (blk, blk), 1)
    blk_row = lax.broadcasted_iota(jnp.int32, (nb, blk), 0)

    key_blk = lax.broadcasted_iota(jnp.int32, (S, LANE), 0) // blk
    onehot = (key_blk == lax.broadcasted_iota(jnp.int32, (S, LANE), 1)).astype(BF16)
    k_aug = jnp.concatenate([k, onehot], axis=1)

    def cols(i):
        return slice(i * blk, (i + 1) * blk)

    def scores(i):
        kv_len = (i + 1) * blk
        if i == 0:
            s = jnp.dot(k[:kv_len], q_t[:, cols(i)], preferred_element_type=F32)
            return jnp.where(key_row <= query_col, s, NEG)
        g = gate[:, cols(i)]
        rank = jnp.zeros((nb, blk), jnp.int32)
        for jp in range(i):
            c = g[jp:jp + 1, :]
            beats = (c > g) | ((c == g) & (blk_row > jp))
            rank = rank + beats.astype(jnp.int32)
        bias = jnp.where((rank >= MOBA_TOPK) & (blk_row < i), NEG, 0.0)
        bias = jnp.concatenate([bias, jnp.zeros((LANE - nb, blk), F32)], axis=0).astype(BF16)
        q_aug = jnp.concatenate([q_t[:, cols(i)], bias], axis=0)
        s = jnp.dot(k_aug[:kv_len], q_aug, preferred_element_type=F32)
        own = jnp.where(key_row <= query_col, s[i * blk:], NEG)
        return jnp.concatenate([s[:i * blk], own], axis=0)

    def finish(i, pv_t):
        out = (pv_t[:HEAD_DIM] / pv_t[HEAD_DIM:HEAD_DIM + 1]).T * z_ref[cols(i), :].astype(F32)
        o_ref[cols(i), :] = out.astype(BF16)

    def pv(i, p):
        return jnp.dot(v_t[:, :(i + 1) * blk], p.astype(BF16), preferred_element_type=F32)

    return nb, scores, pv, finish


def _moba_kernel(q_ref, k_ref, v_ref, z_ref, o_ref):
    scale = HEAD_DIM ** -0.5
    heads = [_moba_head(q_ref.at[h], k_ref.at[h], v_ref.at[h], z_ref.at[h], o_ref.at[h])
             for h in range(q_ref.shape[0])]
    work = [(scores, pv, finish, i) for nb, scores, pv, finish in heads for i in range(nb)]
    s_all = [scores(i) for scores, _, _, i in work]
    m = [_reduce_rows(jnp.max, s) for s in s_all]
    p = [jnp.exp2((s - mi) * (scale * _LOG2E)) for s, mi in zip(s_all, m)]
    pv_t = [pv(i, pi) for (_, pv, _, i), pi in zip(work, p)]
    for (_, _, finish, i), t in zip(work, pv_t):
        finish(i, t)


def _moba(p_slots, offs):
    B, _, S, _ = p_slots.shape
    assert S % MOBA_BLOCK == 0 and S // MOBA_BLOCK <= LANE

    hps = MOBA_HEADS_PER_STEP
    assert MOBA_HEADS % hps == 0 and all(offs[n] % hps == 0 for n in ("qa", "ka", "va", "za"))

    def slab(base):
        return pl.BlockSpec((None, hps, S, LANE), lambda b, h: (b, base // hps + h, 0, 0))

    return pl.pallas_call(
        _moba_kernel,
        grid=(B, MOBA_HEADS // hps),
        in_specs=[slab(offs["qa"]), slab(offs["ka"]), slab(offs["va"]), slab(offs["za"])],
        out_specs=pl.BlockSpec((None, hps, S, LANE), lambda b, h: (b, h, 0, 0)),
        out_shape=jax.ShapeDtypeStruct((B, MOBA_HEADS, S, LANE), BF16),
        compiler_params=pltpu.CompilerParams(
            dimension_semantics=("arbitrary", "arbitrary"),
            vmem_limit_bytes=VMEM_LIMIT_BYTES),
        name="moba",
    )(p_slots, p_slots, p_slots, p_slots)


def _dilated_kernel(q0, q1, q2, k0, k1, k2, v0, v1, v2, z_ref, o_ref, o_scr, l_scr):
    S = z_ref.shape[0]
    scale = HEAD_DIM ** -0.5
    q_refs, k_refs, v_refs = (q0, q1, q2), (k0, k1, k2), (v0, v1, v2)

    band = LANE
    n_blocks = S // band
    row = lax.broadcasted_iota(jnp.int32, (band, 2 * band), 0)
    col = lax.broadcasted_iota(jnp.int32, (band, 2 * band), 1)
    mask_rest = (col >= row) & (col <= row + band)
    mask_first = mask_rest & (col >= band)
    ones = jnp.ones((2 * band, LANE), BF16)

    for g, (window, d) in enumerate(DIL_CONFIGS):
        assert window // d == band
        nblk = S // d // band
        q = q_refs[g][...]
        k = k_refs[g][...]
        v = v_refs[g][...]

        def rows(b):
            return slice(b * band, (b + 1) * band)

        def window_rows(t, b):
            if b % nblk == 0:
                return jnp.concatenate([t[rows(b)], t[rows(b)]], axis=0)
            return t[(b - 1) * band:(b + 1) * band]

        s = jnp.concatenate(
            [jnp.where(mask_first if b % nblk == 0 else mask_rest,
                       lax.dot_general(q[rows(b)], window_rows(k, b), _CONTRACT_LAST,
                                       preferred_element_type=F32), NEG)
             for b in range(n_blocks)], axis=0)
        m = jnp.max(s, axis=1, keepdims=True)
        p = jnp.exp2((s - m) * (scale * _LOG2E))
        p = p.astype(BF16)
        for b in range(n_blocks):
            pv_l = jnp.dot(p[rows(b)], jnp.concatenate([window_rows(v, b), ones], axis=1),
                           preferred_element_type=F32)
            l = pv_l[:, HEAD_DIM:]
            r, n = divmod(b, nblk)
            dst = pl.ds(r + n * band * d, band, stride=d) if d > 1 else pl.ds(b * band, band)
            o_scr[g, dst, :] = pv_l[:, :HEAD_DIM] / l
            l_scr[g, dst, :] = m[rows(b)] * scale + jnp.log(l)

    lses = [l_scr[g] for g in range(len(DIL_CONFIGS))]
    m = functools.reduce(jnp.maximum, lses)
    es = [jnp.exp(l - m) for l in lses]
    den = functools.reduce(lambda a, b: a + b, es)
    num = functools.reduce(lambda a, b: a + b, [(e / den) * o_scr[g] for g, e in enumerate(es)])
    o_ref[...] = (num * z_ref[...].astype(F32)).astype(BF16)


def _dilated(p_slots, offs):
    B, _, S, _ = p_slots.shape
    G = len(DIL_CONFIGS)
    hg = DIL_HEADS_PER_GROUP
    for window, d in DIL_CONFIGS:
        band = window // d
        assert band == LANE and S % (d * band) == 0

    def slab(base):
        return pl.BlockSpec((None, None, S, LANE), lambda b, h: (b, base + h, 0, 0))

    in_specs = ([slab(offs[f"{name}{g}"]) for name in ("qb", "kb", "vb") for g in range(G)]
                + [slab(offs["zb"])])
    return pl.pallas_call(
        _dilated_kernel,
        grid=(B, hg),
        in_specs=in_specs,
        out_specs=pl.BlockSpec((None, None, S, LANE), lambda b, h: (b, h, 0, 0)),
        out_shape=jax.ShapeDtypeStruct((B, hg, S, LANE), BF16),
        scratch_shapes=[pltpu.VMEM((G, S, LANE), F32)] * 2,
        compiler_params=pltpu.CompilerParams(
            dimension_semantics=("arbitrary", "arbitrary"),
            vmem_limit_bytes=VMEM_LIMIT_BYTES),
        name="dilated",
    )(*([p_slots] * (3 * G + 1)))


def _mem_kv_kernel(mem_ref, g_ref, w_ref, o_ref, wb_ref):
    @pl.when(pl.program_id(0) == 0)
    def _():
        wb_ref[...] = w_ref[...].astype(BF16)

    h = _rms_norm(mem_ref[...], g_ref[...]).astype(BF16)
    o_ref[...] = jnp.dot(h, wb_ref[...], preferred_element_type=F32).astype(BF16)


def _mem_kv(mem, g, w):
    B, M, D = mem.shape
    width = w.shape[1]
    return pl.pallas_call(
        _mem_kv_kernel,
        grid=(B,),
        in_specs=[
            pl.BlockSpec((None, M, D), lambda b: (b, 0, 0)),
            pl.BlockSpec((1, D), lambda b: (0, 0)),
            pl.BlockSpec((D, width), lambda b: (0, 0), pipeline_mode=pl.Buffered(1)),
        ],
        out_specs=pl.BlockSpec((None, M, width), lambda b: (b, 0, 0)),
        out_shape=jax.ShapeDtypeStruct((B, M, width), BF16),
        scratch_shapes=[pltpu.VMEM((D, width), BF16)],
        compiler_params=pltpu.CompilerParams(
            dimension_semantics=("arbitrary",),
            vmem_limit_bytes=VMEM_LIMIT_BYTES),
        name="mem_kv",
    )(mem, g.reshape(1, D), w)


def _load_weights_bf16(pairs, stage_ref, sems):
    rows = stage_ref.shape[1]
    chunks = [(src, dst, r0) for src, dst in pairs for r0 in range(0, src.shape[0], rows)]

    def copy(n):
        src, _, r0 = chunks[n]
        return pltpu.make_async_copy(src.at[pl.ds(r0, rows)], stage_ref.at[n % 2], sems.at[n % 2])

    copy(0).start()
    for n, (_, dst, r0) in enumerate(chunks):
        if n + 1 < len(chunks):
            copy(n + 1).start()
        copy(n).wait()
        dst[pl.ds(r0, rows), :] = stage_ref[n % 2].astype(BF16)


def _out_proj_kernel(x_ref, ya_ref, yb_ref, qm_ref, zm_ref, ga0, ga1, gb0, gb1, gm0, gm1, mkv_ref,
                     wa_hbm, wb_hbm, wm_hbm, wo_hbm, gf_ref, o_ref,
                     wa_ref, wb_ref, wm_ref, wo_ref, stage_ref, w_sems, *, final_norm):
    @pl.when((pl.program_id(0) == 0) & (pl.program_id(1) == 0))
    def _():
        _load_weights_bf16(((wa_hbm, wa_ref), (wb_hbm, wb_ref), (wm_hbm, wm_ref), (wo_hbm, wo_ref)),
                           stage_ref, w_sems)

    def slots(ref):
        return jnp.concatenate([ref[s] for s in range(ref.shape[0])], axis=1)

    def gate(r0, r1):
        return jnp.concatenate([slots(r0), slots(r1)], axis=1).astype(F32)

    merged = (gate(ga0, ga1) * jnp.dot(slots(ya_ref), wa_ref[...], preferred_element_type=F32)
              + gate(gb0, gb1) * jnp.dot(slots(yb_ref), wb_ref[...], preferred_element_type=F32))

    w_m = MEM_HEADS * MEM_HEAD_DIM
    scale = MEM_HEAD_DIM ** -0.5
    qm = slots(qm_ref)
    zm = slots(zm_ref)
    heads = [slice(h * MEM_HEAD_DIM, (h + 1) * MEM_HEAD_DIM) for h in range(MEM_HEADS)]
    s = [lax.dot_general(qm[:, c], mkv_ref[:, c], _CONTRACT_LAST, preferred_element_type=F32) for c in heads]
    m = [jnp.max(si, axis=1, keepdims=True) for si in s]
    p = [jnp.exp2((si - mi) * (scale * _LOG2E)) for si, mi in zip(s, m)]
    l = [jnp.sum(pi, axis=1, keepdims=True) for pi in p]
    pv = [jnp.dot(pi.astype(BF16), mkv_ref[:, w_m + c.start:w_m + c.stop], preferred_element_type=F32)
          for pi, c in zip(p, heads)]
    ym = jnp.concatenate([((pvi / li) * zm[:, c].astype(F32)).astype(BF16)
                          for pvi, li, c in zip(pv, l, heads)], axis=1)
    merged = merged + gate(gm0, gm1) * jnp.dot(ym, wm_ref[...], preferred_element_type=F32)
    y = x_ref[...] + jnp.dot(merged.astype(BF16), wo_ref[...], preferred_element_type=F32)
    o_ref[...] = _rms_norm(y, gf_ref[...]) if final_norm else y


def _out_proj(x, ya, yb, p_slots, offs, mkv, wa, wb, wm, wo, g_final, final_norm):
    B, S, D = x.shape
    tm = OUT_TM
    gslots = D // LANE // 2
    assert S % tm == 0 and offs["gates"] % gslots == 0 and gslots == MOBA_HEADS
    assert offs["qm"] % gslots == 0 and offs["zm"] % gslots == 0

    def pblock(slot0):
        blk = slot0 // gslots
        return pl.BlockSpec((None, gslots, tm, LANE), lambda b, s: (b, blk, s, 0))

    weights = (wa, wb, wm, wo)
    assert all(w.shape[1] == D and w.shape[0] % OUT_W_ROWS == 0 for w in weights)
    gate_specs = [pblock(offs["gates"] + i * gslots) for i in range(2 * N_BRANCH)]
    return pl.pallas_call(
        functools.partial(_out_proj_kernel, final_norm=final_norm),
        grid=(B, S // tm),
        in_specs=[
            pl.BlockSpec((None, tm, D), lambda b, s: (b, s, 0)),
            pl.BlockSpec((None, MOBA_HEADS, tm, LANE), lambda b, s: (b, 0, s, 0)),
            pl.BlockSpec((None, DIL_HEADS_PER_GROUP, tm, LANE), lambda b, s: (b, 0, s, 0)),
            pblock(offs["qm"]), pblock(offs["zm"]), *gate_specs,
            pl.BlockSpec((None,) + mkv.shape[1:], lambda b, s: (b, 0, 0)),
            *([pl.BlockSpec(memory_space=pltpu.HBM)] * len(weights)),
            pl.BlockSpec((1, D), lambda b, s: (0, 0)),
        ],
        out_specs=pl.BlockSpec((None, tm, D), lambda b, s: (b, s, 0)),
        out_shape=jax.ShapeDtypeStruct((B, S, D), F32),
        scratch_shapes=[pltpu.VMEM(w.shape, BF16) for w in weights]
        + [pltpu.VMEM((2, OUT_W_ROWS, D), F32), pltpu.SemaphoreType.DMA((2,))],
        compiler_params=pltpu.CompilerParams(
            dimension_semantics=("arbitrary", "arbitrary"),
            vmem_limit_bytes=VMEM_LIMIT_BYTES),
        name="out_proj",
    )(x, ya, yb, *([p_slots] * (2 + 2 * N_BRANCH)), mkv, wa, wb, wm, wo, g_final.reshape(1, D))


def _rope_tables(S):
    half = HEAD_DIM // 2
    inv = ROPE_THETA ** (-jnp.arange(half, dtype=F32) / half)
    ang = jnp.arange(S, dtype=F32)[:, None] * inv[None, :]
    cos, sin = jnp.cos(ang), jnp.sin(ang)
    return jnp.concatenate([cos, cos], axis=1), jnp.concatenate([-sin, sin], axis=1)


def kernel(x, mem, norm_in_g, norm_mem_g, w_in, w_mem_kv, w_proj_a, w_proj_b, w_proj_m, w_out, norm_final_g):
    B, S, D = x.shape
    depth = w_in.shape[0]
    offs, n_slots = _slot_offsets(D)
    assert w_in.shape[2] == n_slots * LANE
    cos_t, sin_t = _rope_tables(S)
    for layer in range(depth):
        p_slots = _in_proj(x, norm_in_g[layer], w_in[layer], cos_t, sin_t)
        ya = _moba(p_slots, offs)
        yb = _dilated(p_slots, offs)
        mkv = _mem_kv(mem, norm_mem_g, w_mem_kv[layer])
        x = _out_proj(x, ya, yb, p_slots, offs, mkv,
                      w_proj_a[layer], w_proj_b[layer], w_proj_m[layer], w_out[layer],
                      norm_final_g, final_norm=(layer == depth - 1))
    return x
```
